```python
import math
import jax, jax.numpy as jnp
from jax import lax
import numpy as np

D_MODEL = 1024
BATCH = 2
SEQ = 16384
DEPTH = 4

N_MIXERS = 2
N_CONV_LAYERS = (DEPTH + 1) // 2
N_ATTN_LAYERS = DEPTH // 2
D_FF = 2816
CONV_WIDTH = 3
HEAD_DIM = 64
N_Q_HEADS = D_MODEL // HEAD_DIM
N_KV_HEADS = 4
GROUP = N_Q_HEADS // N_KV_HEADS
WINDOW = 128
BLOCK = 128
ROPE_THETA = 10000.0
RMS_EPS = 1e-6

kernel_name = "hybrid_shortconv_swa_sink_macaron"


def rmsnorm(x, g):
    x32 = x.astype(jnp.float32)
    y = x32 * lax.rsqrt(jnp.mean(x32 * x32, axis=-1, keepdims=True) + RMS_EPS)
    return y.astype(x.dtype) * g


def swiglu(x, w_in, w_out):
    gate, up = jnp.split(x @ w_in, 2, axis=-1)
    return (jax.nn.silu(gate) * up) @ w_out


def short_conv_mixer(h, w_in, conv_k, w_out):
    S = h.shape[1]
    gate_b, gate_c, u = jnp.split(h @ w_in, 3, axis=-1)
    z = gate_c * u
    zp = jnp.pad(z, ((0, 0), (CONV_WIDTH - 1, 0), (0, 0)))
    conv = conv_k[0] * zp[:, 0:S]
    for tap in range(1, CONV_WIDTH):
        conv = conv + conv_k[tap] * zp[:, tap:tap + S]
    return (gate_b * conv) @ w_out


def rope_tables(positions, dtype):
    inv_freq = ROPE_THETA ** (-jnp.arange(0, HEAD_DIM, 2, dtype=jnp.float32) / HEAD_DIM)
    ang = positions.astype(jnp.float32)[..., None] * inv_freq
    return jnp.cos(ang)[:, :, None, :].astype(dtype), jnp.sin(ang)[:, :, None, :].astype(dtype)


def apply_rope(t, cos, sin):
    t1, t2 = jnp.split(t, 2, axis=-1)
    return jnp.concatenate([t1 * cos - t2 * sin, t2 * cos + t1 * sin], axis=-1)


def banded_keys(t, nb):
    B = t.shape[0]
    tb = t.reshape(B, nb, BLOCK, N_KV_HEADS, HEAD_DIM)
    prev = jnp.concatenate([jnp.zeros_like(tb[:, :1]), tb[:, :-1]], axis=1)
    return jnp.concatenate([prev, tb], axis=2)


def sliding_window_attention(h, positions, w_qkv, sinks, w_o):
    B, S, _ = h.shape
    nb = S // BLOCK
    qkv = h @ w_qkv
    q, k, v = jnp.split(qkv, [N_Q_HEADS * HEAD_DIM, (N_Q_HEADS + N_KV_HEADS) * HEAD_DIM], axis=-1)
    q = q.reshape(B, S, N_Q_HEADS, HEAD_DIM)
    k = k.reshape(B, S, N_KV_HEADS, HEAD_DIM)
    v = v.reshape(B, S, N_KV_HEADS, HEAD_DIM)
    cos, sin = rope_tables(positions, h.dtype)
    q = apply_rope(q, cos, sin)
    k = apply_rope(k, cos, sin)

    qb = q.reshape(B, nb, BLOCK, N_KV_HEADS, GROUP, HEAD_DIM)
    kw = banded_keys(k, nb)
    vw = banded_keys(v, nb)
    scale = 1.0 / math.sqrt(HEAD_DIM)
    s = jnp.einsum('bnqkgd,bnskd->bnkgqs', qb, kw).astype(jnp.float32) * scale

    qi = jnp.arange(BLOCK)[:, None]
    kj = jnp.arange(2 * BLOCK)[None, :]
    dist = qi + BLOCK - kj
    in_band = (dist >= 0) & (dist < WINDOW)
    blk = jnp.arange(nb)[:, None, None]
    mask = in_band[None] & ((blk > 0) | (kj >= BLOCK)[None])
    s = jnp.where(mask[None, :, None, None], s, jnp.finfo(jnp.float32).min)

    sink = jnp.broadcast_to(sinks.astype(jnp.float32).reshape(1, 1, N_KV_HEADS, GROUP, 1, 1),
                            s.shape[:-1] + (1,))
    p = jax.nn.softmax(jnp.concatenate([s, sink], axis=-1), axis=-1)[..., :-1].astype(h.dtype)
    o = jnp.einsum('bnkgqs,bnskd->bnqkgd', p, vw).reshape(B, S, N_Q_HEADS * HEAD_DIM)
    return o @ w_o


def setup_inputs(seed: int = 0) -> dict:
    key = jax.random.key(seed)
    ks = jax.random.split(key, 20)
    D, F = D_MODEL, D_FF
    qkv_w = (N_Q_HEADS + 2 * N_KV_HEADS) * HEAD_DIM

    def nrm(k, shape, fan_in):
        return jax.random.normal(k, shape, jnp.float32) * (fan_in ** -0.5)

    def gain(k, shape):
        return 1.0 + 0.05 * jax.random.normal(k, shape, jnp.float32)

    x = jax.random.normal(ks[0], (BATCH, SEQ, D), jnp.float32)
    positions = jnp.broadcast_to(jnp.arange(SEQ, dtype=jnp.int32)[None, :], (BATCH, SEQ))
    return {
        "x": x,
        "positions": positions,
        "ln_ff1": gain(ks[1], (DEPTH, D)),
        "w_ff1_in": nrm(ks[2], (DEPTH, D, 2 * F), D),
        "w_ff1_out": nrm(ks[3], (DEPTH, F, D), F),
        "ln_mix": gain(ks[4], (DEPTH, D)),
        "ln_ff2": gain(ks[5], (DEPTH, D)),
        "w_ff2_in": nrm(ks[6], (DEPTH, D, 2 * F), D),
        "w_ff2_out": nrm(ks[7], (DEPTH, F, D), F),
        "conv_w_in": nrm(ks[8], (N_CONV_LAYERS, D, 3 * D), D),
        "conv_kernel": nrm(ks[9], (N_CONV_LAYERS, CONV_WIDTH, D), CONV_WIDTH),
        "conv_w_out": nrm(ks[10], (N_CONV_LAYERS, D, D), D),
        "attn_w_qkv": nrm(ks[11], (N_ATTN_LAYERS, D, qkv_w), D),
        "attn_sinks": 0.5 * jax.random.normal(ks[12], (N_ATTN_LAYERS, N_Q_HEADS), jnp.float32),
        "attn_w_o": nrm(ks[13], (N_ATTN_LAYERS, N_Q_HEADS * HEAD_DIM, D), N_Q_HEADS * HEAD_DIM),
        "ln_final": gain(ks[14], (D,)),
    }


def reference(x, positions, ln_ff1, w_ff1_in, w_ff1_out, ln_mix, ln_ff2, w_ff2_in, w_ff2_out,
              conv_w_in, conv_kernel, conv_w_out, attn_w_qkv, attn_sinks, attn_w_o, ln_final):
    for i in range(DEPTH):
        x = x + 0.5 * swiglu(rmsnorm(x, ln_ff1[i]), w_ff1_in[i], w_ff1_out[i])
        h = rmsnorm(x, ln_mix[i])
        j = i // N_MIXERS
        if i % N_MIXERS == 0:
            x = x + short_conv_mixer(h, conv_w_in[j], conv_kernel[j], conv_w_out[j])
        else:
            x = x + sliding_window_attention(h, positions, attn_w_qkv[j], attn_sinks[j], attn_w_o[j])
        x = x + 0.5 * swiglu(rmsnorm(x, ln_ff2[i]), w_ff2_in[i], w_ff2_out[i])
    return rmsnorm(x, ln_final)
```

```python
import functools
import math

import jax
import jax.numpy as jnp
from jax import lax
from jax.experimental import pallas as pl
from jax.experimental.pallas import tpu as pltpu

RMS_EPS = 1e-6
ROPE_THETA = 10000.0
HEAD_DIM = 64
N_Q_HEADS = 16
N_KV_HEADS = 4
GROUP = N_Q_HEADS // N_KV_HEADS
WINDOW = 128
CONV_WIDTH = 3
LANES = 128
MXU_DIM = 256

ROW_TILE = 512
VMEM_LIMIT_BYTES = 56 * 1024 * 1024

F32 = jnp.float32
BF16 = jnp.bfloat16


def _rmsnorm(x, g):
    ms = jnp.mean(x * x, axis=-1, keepdims=True)
    return (x * lax.rsqrt(ms + RMS_EPS)) * g


def _resident(shape):
    return pl.BlockSpec(shape, lambda i: (0,) * len(shape), pipeline_mode=pl.Buffered(1))


def _row_spec(tm, width):
    return pl.BlockSpec((tm, width), lambda i: (i, 0))


def _params(semantics):
    return pltpu.CompilerParams(dimension_semantics=(semantics,),
                                vmem_limit_bytes=VMEM_LIMIT_BYTES)


def _ffn_kernel(*refs, n_chunks, fc, final_norm):
    if final_norm:
        x_ref, g_ref, wgu_ref, wo_ref, gf_ref, o_ref, xn_ref, h_ref = refs
    else:
        x_ref, g_ref, wgu_ref, wo_ref, o_ref, xn_ref, h_ref = refs
    xn_ref[...] = _rmsnorm(x_ref[...], g_ref[...]).astype(BF16)
    for j in range(n_chunks):
        gu = jnp.dot(xn_ref[...], wgu_ref[:, j * 2 * fc:(j + 1) * 2 * fc],
                     preferred_element_type=F32)
        gate = gu[:, :fc]
        up = gu[:, fc:]
        h_ref[:, j * fc:(j + 1) * fc] = ((gate / (1.0 + jnp.exp(-gate))) * up).astype(BF16)
    y = jnp.dot(h_ref[...], wo_ref[...], preferred_element_type=F32)
    out = x_ref[...] + 0.5 * y
    if final_norm:
        out = _rmsnorm(out, gf_ref[...])
    o_ref[...] = out


def _ffn(x, gain, w_in, w_out, final_gain=None):
    n, d = x.shape
    f = w_out.shape[0]
    fc = MXU_DIM
    n_chunks = f // fc
    assert n_chunks * fc == f and n % ROW_TILE == 0
    tm = ROW_TILE
    wgu = w_in.reshape(d, 2, n_chunks, fc).transpose(0, 2, 1, 3).reshape(d, 2 * f).astype(BF16)
    wo = w_out.astype(BF16)
    final_norm = final_gain is not None
    in_specs = [_row_spec(tm, d), _resident((1, d)), _resident((d, 2 * f)), _resident((f, d))]
    args = [x, gain.reshape(1, d), wgu, wo]
    if final_norm:
        in_specs.append(_resident((1, d)))
        args.append(final_gain.reshape(1, d))
    return pl.pallas_call(
        functools.partial(_ffn_kernel, n_chunks=n_chunks, fc=fc, final_norm=final_norm),
        grid=(n // tm,),
        in_specs=in_specs,
        out_specs=_row_spec(tm, d),
        out_shape=jax.ShapeDtypeStruct((n, d), F32),
        scratch_shapes=[pltpu.VMEM((tm, d), BF16), pltpu.VMEM((tm, f), BF16)],
        compiler_params=_params("parallel"),
        name="ffn_final" if final_norm else "ffn",
    )(*args)


def _conv_kernel(x_ref, g_ref, win_ref, ck_ref, wout_ref, o_ref, xn_ref, hc_ref, carry_ref,
                 *, n_chunks, cc, tiles_per_seq):
    tm = x_ref.shape[0]
    i = pl.program_id(0)

    @pl.when(i % tiles_per_seq == 0)
    def _():
        carry_ref[...] = jnp.zeros_like(carry_ref)

    xn_ref[...] = _rmsnorm(x_ref[...], g_ref[...]).astype(BF16)
    row = lax.broadcasted_iota(jnp.int32, (tm, cc), 0)
    for j in range(n_chunks):
        bcu = jnp.dot(xn_ref[...], win_ref[:, j * 3 * cc:(j + 1) * 3 * cc],
                      preferred_element_type=F32)
        gate_b = bcu[:, :cc]
        z = bcu[:, cc:2 * cc] * bcu[:, 2 * cc:]
        prev = carry_ref[j]
        carry_ref[j] = z[tm - 8:, :]
        p1 = jnp.broadcast_to(prev[7:8, :], (tm, cc))
        p2 = jnp.broadcast_to(prev[6:7, :], (tm, cc))
        z1 = jnp.where(row == 0, p1, pltpu.roll(z, 1, 0))
        z2 = jnp.where(row == 0, p2, jnp.where(row == 1, p1, pltpu.roll(z, 2, 0)))
        k = ck_ref[:, j * cc:(j + 1) * cc]
        conv = k[0:1, :] * z2
        conv = conv + k[1:2, :] * z1
        conv = conv + k[2:3, :] * z
        hc_ref[:, j * cc:(j + 1) * cc] = (gate_b * conv).astype(BF16)
    y = jnp.dot(hc_ref[...], wout_ref[...], preferred_element_type=F32)
    o_ref[...] = x_ref[...] + y


def _conv_mixer(x, gain, w_in, conv_k, w_out, seq_len):
    n, d = x.shape
    tm = ROW_TILE
    cc = MXU_DIM
    n_chunks = d // cc
    assert seq_len % tm == 0 and n % seq_len == 0 and CONV_WIDTH - 1 <= 8
    win = w_in.reshape(d, 3, n_chunks, cc).transpose(0, 2, 1, 3).reshape(d, 3 * d).astype(BF16)
    return pl.pallas_call(
        functools.partial(_conv_kernel, n_chunks=n_chunks, cc=cc, tiles_per_seq=seq_len // tm),
        grid=(n // tm,),
        in_specs=[_row_spec(tm, d), _resident((1, d)), _resident((d, 3 * d)),
                  _resident((CONV_WIDTH, d)), _resident((d, d))],
        out_specs=_row_spec(tm, d),
        out_shape=jax.ShapeDtypeStruct((n, d), F32),
        scratch_shapes=[pltpu.VMEM((tm, d), BF16), pltpu.VMEM((tm, d), BF16),
                        pltpu.VMEM((n_chunks, 8, cc), F32)],
        compiler_params=_params("arbitrary"),
        name="conv_mixer",
    )(x, gain.reshape(1, d), win, conv_k, w_out.astype(BF16))


def _rope_kernel(pos_ref, invf_ref, cos_ref, sin_ref):
    ang = pos_ref[...].astype(F32) * invf_ref[...]
    lane = lax.broadcasted_iota(jnp.int32, ang.shape, 1)
    sign = jnp.where((lane & (HEAD_DIM // 2)) == 0, -1.0, 1.0)
    cos_ref[...] = jnp.cos(ang)
    sin_ref[...] = jnp.sin(ang) * sign


def _rope_tables(positions_flat):
    n = positions_flat.shape[0]
    tm = ROW_TILE
    inv_freq = ROPE_THETA ** (-jnp.arange(0, HEAD_DIM, 2, dtype=F32) / HEAD_DIM)
    invf = jnp.tile(inv_freq, LANES // (HEAD_DIM // 2)).reshape(1, LANES)
    pos = jnp.broadcast_to(positions_flat.reshape(n, 1), (n, LANES))
    return pl.pallas_call(
        _rope_kernel,
        grid=(n // tm,),
        in_specs=[_row_spec(tm, LANES), _resident((1, LANES))],
        out_specs=[_row_spec(tm, LANES), _row_spec(tm, LANES)],
        out_shape=[jax.ShapeDtypeStruct((n, LANES), F32)] * 2,
        compiler_params=_params("parallel"),
        name="rope_tables",
    )(pos, invf)


def _attn_kernel(sinks_ref, x_ref, g_ref, cos_ref, sin_ref, wqkv_ref, wo_ref, o_ref,
                 xn_ref, q_ref, ka_ref, kb_ref, va_ref, vb_ref, att_ref, *, tiles_per_seq):
    tm, d = x_ref.shape
    blocks = tm // WINDOW
    i = pl.program_id(0)
    first = i % tiles_per_seq == 0
    kv_w = N_KV_HEADS * HEAD_DIM
    halo = (ka_ref, kb_ref, va_ref, vb_ref)

    @pl.when(first)
    def _():
        for r in halo:
            r[:, 0:WINDOW, :] = jnp.zeros((N_KV_HEADS, WINDOW, LANES), BF16)

    @pl.when(jnp.logical_not(first))
    def _():
        for r in halo:
            r[:, 0:WINDOW, :] = r[:, tm:tm + WINDOW, :]

    xn_ref[...] = _rmsnorm(x_ref[...], g_ref[...]).astype(BF16)
    qkv = jnp.dot(xn_ref[...], wqkv_ref[...], preferred_element_type=F32)

    cos_t = cos_ref[...]
    sin_t = sin_ref[...]
    lane = lax.broadcasted_iota(jnp.int32, (tm, LANES), 1)
    first_half = (lane & (HEAD_DIM // 2)) == 0
    low_head = lane < HEAD_DIM

    def rope(t):
        partner = jnp.where(first_half,
                            pltpu.roll(t, LANES - HEAD_DIM // 2, 1),
                            pltpu.roll(t, HEAD_DIM // 2, 1))
        return t * cos_t + partner * sin_t

    scale = 1.0 / math.sqrt(HEAD_DIM)
    for s in range(d // LANES):
        q_ref[:, s * LANES:(s + 1) * LANES] = (
            rope(qkv[:, s * LANES:(s + 1) * LANES]) * scale).astype(BF16)

    zero = jnp.zeros((tm, LANES), F32)
    for s in range(kv_w // LANES):
        k_slab = rope(qkv[:, d + s * LANES:d + (s + 1) * LANES])
        v_slab = qkv[:, d + kv_w + s * LANES:d + kv_w + (s + 1) * LANES]
        for slab, a_ref, b_ref in ((k_slab, ka_ref, kb_ref), (v_slab, va_ref, vb_ref)):
            swapped = pltpu.roll(slab, HEAD_DIM, 1)
            a_ref[2 * s, WINDOW:, :] = jnp.where(low_head, slab, zero).astype(BF16)
            b_ref[2 * s, WINDOW:, :] = jnp.where(low_head, zero, swapped).astype(BF16)
            a_ref[2 * s + 1, WINDOW:, :] = jnp.where(low_head, swapped, zero).astype(BF16)
            b_ref[2 * s + 1, WINDOW:, :] = jnp.where(low_head, zero, slab).astype(BF16)

    qi = lax.broadcasted_iota(jnp.int32, (WINDOW, WINDOW), 0)
    kj = lax.broadcasted_iota(jnp.int32, (WINDOW, WINDOW), 1)
    cur_ok = kj <= qi
    prev_ok = kj > qi
    prev_ok_first = kj > qi + jnp.where(first, WINDOW, 0)
    low_head_w = lax.broadcasted_iota(jnp.int32, (WINDOW, LANES), 1) < HEAD_DIM
    neg = jnp.finfo(F32).min

    for b in range(blocks):
        rows = slice(b * WINDOW, (b + 1) * WINDOW)
        keys = slice(b * WINDOW, b * WINDOW + 2 * WINDOW)
        mask = jnp.concatenate([prev_ok_first if b == 0 else prev_ok, cur_ok], axis=1)
        for g in range(N_KV_HEADS):
            q2 = jnp.concatenate([q_ref[rows, (2 * g) * LANES:(2 * g + 1) * LANES],
                                  q_ref[rows, (2 * g + 1) * LANES:(2 * g + 2) * LANES]], axis=0)
            k2 = jnp.concatenate([ka_ref[g, keys, :], kb_ref[g, keys, :]], axis=0)
            v2 = jnp.concatenate([va_ref[g, keys, :], vb_ref[g, keys, :]], axis=0)
            s2 = lax.dot_general(q2, k2, (((1,), (1,)), ((), ())),
                                 preferred_element_type=F32)
            p_parts = []
            inv_parts = []
            for pair in range(2):
                p_row = []
                inv_row = []
                for lo in range(2):
                    head = GROUP * g + 2 * pair + lo
                    sink = sinks_ref[head]
                    sc = s2[pair * WINDOW:(pair + 1) * WINDOW,
                            lo * 2 * WINDOW:(lo + 1) * 2 * WINDOW]
                    sc = jnp.where(mask, sc, neg)
                    m = jnp.maximum(jnp.max(sc, axis=1, keepdims=True), sink)
                    p = jnp.exp(sc - m)
                    denom = jnp.sum(p, axis=1, keepdims=True) + jnp.exp(sink - m)
                    p_row.append(p.astype(BF16))
                    inv_row.append(1.0 / denom)
                p_parts.append(jnp.concatenate(p_row, axis=1))
                inv_parts.append(jnp.where(low_head_w, inv_row[0], inv_row[1]))
            p2 = jnp.concatenate(p_parts, axis=0)
            o2 = jnp.dot(p2, v2, preferred_element_type=F32)
            for pair in range(2):
                att_ref[rows, (2 * g + pair) * LANES:(2 * g + pair + 1) * LANES] = (
                    o2[pair * WINDOW:(pair + 1) * WINDOW, :] * inv_parts[pair]).astype(BF16)

    y = jnp.dot(att_ref[...], wo_ref[...], preferred_element_type=F32)
    o_ref[...] = x_ref[...] + y


def _attention(x, gain, cos_t, sin_t, w_qkv, sinks, w_o, seq_len):
    n, d = x.shape
    tm = ROW_TILE
    qkv_w = w_qkv.shape[1]
    assert seq_len % tm == 0 and tm % WINDOW == 0 and d == N_Q_HEADS * HEAD_DIM
    halo_shape = pltpu.VMEM((N_KV_HEADS, tm + WINDOW, LANES), BF16)
    return pl.pallas_call(
        functools.partial(_attn_kernel, tiles_per_seq=seq_len // tm),
        grid=(n // tm,),
        in_specs=[pl.BlockSpec(memory_space=pltpu.SMEM),
                  _row_spec(tm, d), _resident((1, d)), _row_spec(tm, LANES), _row_spec(tm, LANES),
                  _resident((d, qkv_w)), _resident((d, d))],
        out_specs=_row_spec(tm, d),
        out_shape=jax.ShapeDtypeStruct((n, d), F32),
        scratch_shapes=[pltpu.VMEM((tm, d), BF16), pltpu.VMEM((tm, d), BF16),
                        halo_shape, halo_shape, halo_shape, halo_shape,
                        pltpu.VMEM((tm, d), BF16)],
        compiler_params=_params("arbitrary"),
        name="swa_attention",
    )(sinks, x, gain.reshape(1, d), cos_t, sin_t, w_qkv.astype(BF16), w_o.astype(BF16))


def kernel(x, positions, ln_ff1, w_ff1_in, w_ff1_out, ln_mix, ln_ff2, w_ff2_in, w_ff2_out,
           conv_w_in, conv_kernel, conv_w_out, attn_w_qkv, attn_sinks, attn_w_o, ln_final):
    batch, seq_len, d = x.shape
    depth = ln_ff1.shape[0]
    h = x.reshape(batch * seq_len, d)
    cos_t, sin_t = _rope_tables(positions.reshape(batch * seq_len))
    for i in range(depth):
        h = _ffn(h, ln_ff1[i], w_ff1_in[i], w_ff1_out[i])
        j = i // 2
        if i % 2 == 0:
            h = _conv_mixer(h, ln_mix[i], conv_w_in[j], conv_kernel[j], conv_w_out[j], seq_len)
        else:
            h = _attention(h, ln_mix[i], cos_t, sin_t, attn_w_qkv[j], attn_sinks[j],
                           attn_w_o[j], seq_len)
        h = _ffn(h, ln_ff2[i], w_ff2_in[i], w_ff2_out[i],
                 final_gain=ln_final if i == depth - 1 else None)
    return h.reshape(batch, seq_len, d)
```

```python
import functools
import math

import jax
import jax.numpy as jnp
from jax import lax
from jax.experimental import pallas as pl
from jax.experimental.pallas import tpu as pltpu

RMS_EPS = 1e-6
ROPE_THETA = 10000.0
HEAD_DIM = 64
N_Q_HEADS = 16
N_KV_HEADS = 4
GROUP = N_Q_HEADS // N_KV_HEADS
WINDOW = 128
CONV_WIDTH = 3
LANES = 128
MXU_DIM = 256

ROW_TILE = 512
FFN_ROW_TILE = 1024
SCORE_LOOKAHEAD = 2
VMEM_LIMIT_BYTES = 56 * 1024 * 1024

F32 = jnp.float32
BF16 = jnp.bfloat16


def _rmsnorm(x, g):
    ms = jnp.mean(x * x, axis=-1, keepdims=True)
    return (x * lax.rsqrt(ms + RMS_EPS)) * g


def _resident(shape):
    return pl.BlockSpec(shape, lambda i: (0,) * len(shape), pipeline_mode=pl.Buffered(1))


def _resident_layer(shape, layer):
    return pl.BlockSpec((None,) + shape, lambda i: (layer,) + (0,) * len(shape),
                        pipeline_mode=pl.Buffered(1))


def _row_spec(tm, width):
    return pl.BlockSpec((tm, width), lambda i: (i, 0))


def _params(semantics):
    return pltpu.CompilerParams(dimension_semantics=(semantics,),
                                vmem_limit_bytes=VMEM_LIMIT_BYTES)


def _ffn_kernel(*refs, n_chunks, fc, final_norm):
    if final_norm:
        x_ref, g_ref, win_ref, wo_ref, gf_ref, o_ref, xn_ref, h_ref = refs
    else:
        x_ref, g_ref, win_ref, wo_ref, o_ref, xn_ref, h_ref = refs
    f = n_chunks * fc
    xn_ref[...] = _rmsnorm(x_ref[...], g_ref[...]).astype(BF16)
    for j in range(n_chunks):
        gate = jnp.dot(xn_ref[...], win_ref[:, j * fc:(j + 1) * fc], preferred_element_type=F32)
        up = jnp.dot(xn_ref[...], win_ref[:, f + j * fc:f + (j + 1) * fc],
                     preferred_element_type=F32)
        h_ref[:, j * fc:(j + 1) * fc] = ((gate / (1.0 + jnp.exp(-gate))) * up).astype(BF16)
    y = jnp.dot(h_ref[...], wo_ref[...], preferred_element_type=F32)
    out = x_ref[...] + 0.5 * y
    if final_norm:
        out = _rmsnorm(out, gf_ref[...])
    o_ref[...] = out


def _ffn(x, gain, w_in, w_out, layer, final_gain=None):
    n, d = x.shape
    f = w_out.shape[1]
    fc = MXU_DIM
    n_chunks = f // fc
    tm = FFN_ROW_TILE
    assert n_chunks * fc == f and n % tm == 0
    final_norm = final_gain is not None
    in_specs = [_row_spec(tm, d), _resident((1, d)),
                _resident_layer((d, 2 * f), layer), _resident_layer((f, d), layer)]
    args = [x, gain.reshape(1, d), w_in, w_out]
    if final_norm:
        in_specs.append(_resident((1, d)))
        args.append(final_gain.reshape(1, d))
    return pl.pallas_call(
        functools.partial(_ffn_kernel, n_chunks=n_chunks, fc=fc, final_norm=final_norm),
        grid=(n // tm,),
        in_specs=in_specs,
        out_specs=_row_spec(tm, d),
        out_shape=jax.ShapeDtypeStruct((n, d), F32),
        scratch_shapes=[pltpu.VMEM((tm, d), BF16), pltpu.VMEM((tm, f), BF16)],
        compiler_params=_params("parallel"),
        name="ffn_final" if final_norm else "ffn",
    )(*args)


def _conv_kernel(x_ref, g_ref, win_ref, ck_ref, wout_ref, o_ref, xn_ref, hc_ref, carry_ref,
                 *, n_chunks, cc, tiles_per_seq):
    tm = x_ref.shape[0]
    i = pl.program_id(0)

    @pl.when(i % tiles_per_seq == 0)
    def _():
        carry_ref[...] = jnp.zeros_like(carry_ref)

    xn_ref[...] = _rmsnorm(x_ref[...], g_ref[...]).astype(BF16)
    row = lax.broadcasted_iota(jnp.int32, (tm, cc), 0)
    d = n_chunks * cc
    for j in range(n_chunks):
        cols = slice(j * cc, (j + 1) * cc)
        gate_b, gate_c, u = (
            jnp.dot(xn_ref[...], win_ref[:, part * d + j * cc:part * d + (j + 1) * cc],
                    preferred_element_type=F32) for part in range(3))
        z = gate_c * u
        prev = carry_ref[j]
        carry_ref[j] = z[tm - 8:, :]
        p1 = jnp.broadcast_to(prev[7:8, :], (tm, cc))
        p2 = jnp.broadcast_to(prev[6:7, :], (tm, cc))
        z1 = jnp.where(row == 0, p1, pltpu.roll(z, 1, 0))
        z2 = jnp.where(row == 0, p2, jnp.where(row == 1, p1, pltpu.roll(z, 2, 0)))
        k = ck_ref[:, cols]
        conv = k[0:1, :] * z2
        conv = conv + k[1:2, :] * z1
        conv = conv + k[2:3, :] * z
        hc_ref[:, cols] = (gate_b * conv).astype(BF16)
    y = jnp.dot(hc_ref[...], wout_ref[...], preferred_element_type=F32)
    o_ref[...] = x_ref[...] + y


def _conv_mixer(x, gain, w_in, conv_k, w_out, layer, seq_len):
    n, d = x.shape
    tm = ROW_TILE
    cc = MXU_DIM
    n_chunks = d // cc
    assert seq_len % tm == 0 and n % seq_len == 0 and CONV_WIDTH - 1 <= 8
    return pl.pallas_call(
        functools.partial(_conv_kernel, n_chunks=n_chunks, cc=cc, tiles_per_seq=seq_len // tm),
        grid=(n // tm,),
        in_specs=[_row_spec(tm, d), _resident((1, d)), _resident_layer((d, 3 * d), layer),
                  _resident_layer((CONV_WIDTH, d), layer), _resident_layer((d, d), layer)],
        out_specs=_row_spec(tm, d),
        out_shape=jax.ShapeDtypeStruct((n, d), F32),
        scratch_shapes=[pltpu.VMEM((tm, d), BF16), pltpu.VMEM((tm, d), BF16),
                        pltpu.VMEM((n_chunks, 8, cc), F32)],
        compiler_params=_params("arbitrary"),
        name="conv_mixer",
    )(x, gain.reshape(1, d), w_in, conv_k, w_out)


def _rope_kernel(pos_ref, invf_ref, cos_ref, sin_ref):
    ang = pos_ref[...].astype(F32) * invf_ref[...]
    lane = lax.broadcasted_iota(jnp.int32, ang.shape, 1)
    sign = jnp.where((lane & (HEAD_DIM // 2)) == 0, -1.0, 1.0)
    cos_ref[...] = jnp.cos(ang)
    sin_ref[...] = jnp.sin(ang) * sign


def _rope_tables(positions_flat):
    n = positions_flat.shape[0]
    tm = ROW_TILE
    inv_freq = ROPE_THETA ** (-jnp.arange(0, HEAD_DIM, 2, dtype=F32) / HEAD_DIM)
    invf = jnp.tile(inv_freq, LANES // (HEAD_DIM // 2)).reshape(1, LANES)
    pos = jnp.broadcast_to(positions_flat.reshape(n, 1), (n, LANES))
    return pl.pallas_call(
        _rope_kernel,
        grid=(n // tm,),
        in_specs=[_row_spec(tm, LANES), _resident((1, LANES))],
        out_specs=[_row_spec(tm, LANES), _row_spec(tm, LANES)],
        out_shape=[jax.ShapeDtypeStruct((n, LANES), F32)] * 2,
        compiler_params=_params("parallel"),
        name="rope_tables",
    )(pos, invf)


def _transpose_blocks(t):
    return jnp.concatenate([t[r:r + WINDOW, :].T for r in range(0, t.shape[0], WINDOW)], axis=1)


def _attn_kernel(sinks_ref, x_ref, g_ref, cos_ref, sin_ref, wqkv_ref, wo_ref, o_ref,
                 xn_ref, q_ref, ka_ref, kb_ref, vat_ref, vbt_ref, att_ref, *, tiles_per_seq):
    tm, d = x_ref.shape
    blocks = tm // WINDOW
    i = pl.program_id(0)
    first = i % tiles_per_seq == 0
    kv_w = N_KV_HEADS * HEAD_DIM

    @pl.when(first)
    def _():
        for r in (ka_ref, kb_ref):
            r[:, 0:WINDOW, :] = jnp.zeros((N_KV_HEADS, WINDOW, LANES), BF16)
        for r in (vat_ref, vbt_ref):
            r[:, :, 0:WINDOW] = jnp.zeros((N_KV_HEADS, LANES, WINDOW), BF16)

    @pl.when(jnp.logical_not(first))
    def _():
        for r in (ka_ref, kb_ref):
            r[:, 0:WINDOW, :] = r[:, tm:tm + WINDOW, :]
        for r in (vat_ref, vbt_ref):
            r[:, :, 0:WINDOW] = r[:, :, tm:tm + WINDOW]

    xn_ref[...] = _rmsnorm(x_ref[...], g_ref[...]).astype(BF16)
    qkv = jnp.dot(xn_ref[...], wqkv_ref[...], preferred_element_type=F32)

    cos_t = cos_ref[...]
    sin_t = sin_ref[...]
    lane = lax.broadcasted_iota(jnp.int32, (tm, LANES), 1)
    first_half = (lane & (HEAD_DIM // 2)) == 0
    low_head = lane < HEAD_DIM

    def rope(t):
        partner = jnp.where(first_half,
                            pltpu.roll(t, LANES - HEAD_DIM // 2, 1),
                            pltpu.roll(t, HEAD_DIM // 2, 1))
        return t * cos_t + partner * sin_t

    scale = 1.0 / math.sqrt(HEAD_DIM)
    for s in range(d // LANES):
        q_ref[:, s * LANES:(s + 1) * LANES] = (
            rope(qkv[:, s * LANES:(s + 1) * LANES]) * scale).astype(BF16)

    zero = jnp.zeros((tm, LANES), F32)
    zero_t = jnp.zeros((HEAD_DIM, tm), BF16)
    for s in range(kv_w // LANES):
        k_slab = rope(qkv[:, d + s * LANES:d + (s + 1) * LANES])
        swapped = pltpu.roll(k_slab, HEAD_DIM, 1)
        ka_ref[2 * s, WINDOW:, :] = jnp.where(low_head, k_slab, zero).astype(BF16)
        kb_ref[2 * s, WINDOW:, :] = jnp.where(low_head, zero, swapped).astype(BF16)
        ka_ref[2 * s + 1, WINDOW:, :] = jnp.where(low_head, swapped, zero).astype(BF16)
        kb_ref[2 * s + 1, WINDOW:, :] = jnp.where(low_head, zero, k_slab).astype(BF16)
        v_t = _transpose_blocks(
            qkv[:, d + kv_w + s * LANES:d + kv_w + (s + 1) * LANES]).astype(BF16)
        for head, v_head in ((2 * s, v_t[:HEAD_DIM, :]), (2 * s + 1, v_t[HEAD_DIM:, :])):
            vat_ref[head, :, WINDOW:] = jnp.concatenate([v_head, zero_t], axis=0)
            vbt_ref[head, :, WINDOW:] = jnp.concatenate([zero_t, v_head], axis=0)

    kj = lax.broadcasted_iota(jnp.int32, (WINDOW, WINDOW), 0)
    qi = lax.broadcasted_iota(jnp.int32, (WINDOW, WINDOW), 1)
    cur_ok = kj <= qi
    prev_ok = kj > qi
    prev_ok_first = kj > qi + jnp.where(first, WINDOW, 0)
    neg = jnp.finfo(F32).min

    def scores(b, g):
        rows = slice(b * WINDOW, (b + 1) * WINDOW)
        keys = slice(b * WINDOW, b * WINDOW + 2 * WINDOW)
        q2 = jnp.concatenate([q_ref[rows, sl * LANES:(sl + 1) * LANES]
                              for sl in (2 * g, 2 * g + 1)], axis=0)
        k2 = jnp.concatenate([ka_ref[g, keys, :], kb_ref[g, keys, :]], axis=0)
        return lax.dot_general(k2, q2, (((1,), (1,)), ((), ())),
                               preferred_element_type=F32)

    def attend(b, g, s_t):
        rows = slice(b * WINDOW, (b + 1) * WINDOW)
        keys = slice(b * WINDOW, b * WINDOW + 2 * WINDOW)
        mask = jnp.concatenate([prev_ok_first if b == 0 else prev_ok, cur_ok], axis=0)
        p_cols = []
        inv = []
        for pair in range(2):
            p_col = []
            inv_col = []
            for lo in range(2):
                sink = sinks_ref[GROUP * g + 2 * pair + lo]
                sc = s_t[lo * 2 * WINDOW:(lo + 1) * 2 * WINDOW, pair * WINDOW:(pair + 1) * WINDOW]
                sc = jnp.where(mask, sc, neg)
                m = jnp.maximum(jnp.max(sc, axis=0, keepdims=True), sink)
                p = jnp.exp(sc - m)
                denom = jnp.sum(p, axis=0, keepdims=True) + jnp.exp(sink - m)
                p_col.append(p.astype(BF16))
                inv_col.append(1.0 / denom)
            p_cols.append(jnp.concatenate(p_col, axis=0))
            inv.append(inv_col)
        p_t = jnp.concatenate(p_cols, axis=1)
        v2_t = jnp.concatenate([vat_ref[g, :, keys], vbt_ref[g, :, keys]], axis=1)
        o_t = jnp.dot(v2_t, p_t, preferred_element_type=F32)
        for pair in range(2):
            cols = slice(pair * WINDOW, (pair + 1) * WINDOW)
            slab = 2 * g + pair
            o_pair = jnp.concatenate([o_t[:HEAD_DIM, cols] * inv[pair][0],
                                      o_t[HEAD_DIM:, cols] * inv[pair][1]], axis=0)
            att_ref[rows, slab * LANES:(slab + 1) * LANES] = o_pair.T.astype(BF16)

    units = [(b, g) for b in range(blocks) for g in range(N_KV_HEADS)]
    pending = [scores(*u) for u in units[:SCORE_LOOKAHEAD]]
    for n, (b, g) in enumerate(units):
        if n + SCORE_LOOKAHEAD < len(units):
            pending.append(scores(*units[n + SCORE_LOOKAHEAD]))
        attend(b, g, pending[n])

    y = jnp.dot(att_ref[...], wo_ref[...], preferred_element_type=F32)
    o_ref[...] = x_ref[...] + y


def _attention(x, gain, cos_t, sin_t, w_qkv, sinks, w_o, layer, seq_len):
    n, d = x.shape
    tm = ROW_TILE
    qkv_w = w_qkv.shape[2]
    assert seq_len % tm == 0 and tm % WINDOW == 0 and d == N_Q_HEADS * HEAD_DIM
    k_halo = pltpu.VMEM((N_KV_HEADS, tm + WINDOW, LANES), BF16)
    vt_halo = pltpu.VMEM((N_KV_HEADS, LANES, tm + WINDOW), BF16)
    return pl.pallas_call(
        functools.partial(_attn_kernel, tiles_per_seq=seq_len // tm),
        grid=(n // tm,),
        in_specs=[pl.BlockSpec(memory_space=pltpu.SMEM),
                  _row_spec(tm, d), _resident((1, d)), _row_spec(tm, LANES), _row_spec(tm, LANES),
                  _resident_layer((d, qkv_w), layer), _resident_layer((d, d), layer)],
        out_specs=_row_spec(tm, d),
        out_shape=jax.ShapeDtypeStruct((n, d), F32),
        scratch_shapes=[pltpu.VMEM((tm, d), BF16), pltpu.VMEM((tm, d), BF16),
                        k_halo, k_halo, vt_halo, vt_halo,
                        pltpu.VMEM((tm, d), BF16)],
        compiler_params=_params("arbitrary"),
        name="swa_attention",
    )(sinks, x, gain.reshape(1, d), cos_t, sin_t, w_qkv, w_o)


def kernel(x, positions, ln_ff1, w_ff1_in, w_ff1_out, ln_mix, ln_ff2, w_ff2_in, w_ff2_out,
           conv_w_in, conv_kernel, conv_w_out, attn_w_qkv, attn_sinks, attn_w_o, ln_final):
    batch, seq_len, d = x.shape
    depth = ln_ff1.shape[0]
    h = x.reshape(batch * seq_len, d)
    cos_t, sin_t = _rope_tables(positions.reshape(batch * seq_len))
    w_ff1_in, w_ff1_out, w_ff2_in, w_ff2_out, conv_w_in, conv_w_out, attn_w_qkv, attn_w_o = (
        w.astype(BF16) for w in (w_ff1_in, w_ff1_out, w_ff2_in, w_ff2_out,
                                 conv_w_in, conv_w_out, attn_w_qkv, attn_w_o))
    for i in range(depth):
        h = _ffn(h, ln_ff1[i], w_ff1_in, w_ff1_out, i)
        j = i // 2
        if i % 2 == 0:
            h = _conv_mixer(h, ln_mix[i], conv_w_in, conv_kernel, conv_w_out, j, seq_len)
        else:
            h = _attention(h, ln_mix[i], cos_t, sin_t, attn_w_qkv, attn_sinks[j], attn_w_o,
                           j, seq_len)
        h = _ffn(h, ln_ff2[i], w_ff2_in, w_ff2_out, i,
                 final_gain=ln_final if i == depth - 1 else None)
    return h.reshape(batch, seq_len, d)
```

```python
import functools
import math

import jax
import jax.numpy as jnp
from jax import lax
from jax.experimental import pallas as pl
from jax.experimental.pallas import tpu as pltpu

RMS_EPS = 1e-6
ROPE_THETA = 10000.0
HEAD_DIM = 64
N_Q_HEADS = 16
N_KV_HEADS = 4
GROUP = N_Q_HEADS // N_KV_HEADS
WINDOW = 128
CONV_WIDTH = 3
LANES = 128
MXU_DIM = 256

ROW_TILE = 512
CONV_ROW_TILE = 1024
FFN_ROW_TILE = 1024
ATTN_ROW_TILE = 1024
ATTN_PARTS = 4
SCORE_LOOKAHEAD = 2
LOG2_E = math.log2(math.e)
VMEM_LIMIT_BYTES = 56 * 1024 * 1024

F32 = jnp.float32
BF16 = jnp.bfloat16


def _rmsnorm(x, g):
    ms = jnp.mean(x * x, axis=-1, keepdims=True)
    return (x * lax.rsqrt(ms + RMS_EPS)) * g


def _resident(shape):
    return pl.BlockSpec(shape, lambda i: (0,) * len(shape), pipeline_mode=pl.Buffered(1))


def _resident_layer(shape, layer):
    return pl.BlockSpec((None,) + shape, lambda i: (layer,) + (0,) * len(shape),
                        pipeline_mode=pl.Buffered(1))


def _row_spec(tm, width):
    return pl.BlockSpec((tm, width), lambda i: (i, 0))


def _params(semantics):
    return pltpu.CompilerParams(dimension_semantics=(semantics,),
                                vmem_limit_bytes=VMEM_LIMIT_BYTES)


def _ffn_kernel(*refs, n_chunks, fc, final_norm):
    if final_norm:
        x_ref, g_ref, win_ref, wo_ref, gf_ref, o_ref, xn_ref, h_ref = refs
    else:
        x_ref, g_ref, win_ref, wo_ref, o_ref, xn_ref, h_ref = refs
    f = n_chunks * fc
    xn_ref[...] = _rmsnorm(x_ref[...], g_ref[...]).astype(BF16)
    for j in range(n_chunks):
        gate = jnp.dot(xn_ref[...], win_ref[:, j * fc:(j + 1) * fc], preferred_element_type=F32)
        up = jnp.dot(xn_ref[...], win_ref[:, f + j * fc:f + (j + 1) * fc],
                     preferred_element_type=F32)
        h_ref[:, j * fc:(j + 1) * fc] = ((gate / (1.0 + jnp.exp(-gate))) * up).astype(BF16)
    y = jnp.dot(h_ref[...], wo_ref[...], preferred_element_type=F32)
    out = x_ref[...] + 0.5 * y
    if final_norm:
        out = _rmsnorm(out, gf_ref[...])
    o_ref[...] = out


def _ffn(x, gain, w_in, w_out, layer, final_gain=None):
    n, d = x.shape
    f = w_out.shape[1]
    fc = MXU_DIM
    n_chunks = f // fc
    tm = FFN_ROW_TILE
    assert n_chunks * fc == f and n % tm == 0
    final_norm = final_gain is not None
    in_specs = [_row_spec(tm, d), _resident((1, d)),
                _resident_layer((d, 2 * f), layer), _resident_layer((f, d), layer)]
    args = [x, gain.reshape(1, d), w_in, w_out]
    if final_norm:
        in_specs.append(_resident((1, d)))
        args.append(final_gain.reshape(1, d))
    return pl.pallas_call(
        functools.partial(_ffn_kernel, n_chunks=n_chunks, fc=fc, final_norm=final_norm),
        grid=(n // tm,),
        in_specs=in_specs,
        out_specs=_row_spec(tm, d),
        out_shape=jax.ShapeDtypeStruct((n, d), F32),
        scratch_shapes=[pltpu.VMEM((tm, d), BF16), pltpu.VMEM((tm, f), BF16)],
        compiler_params=_params("parallel"),
        name="ffn_final" if final_norm else "ffn",
    )(*args)


def _conv_kernel(x_ref, g_ref, win_ref, ck_ref, wout_ref, o_ref, xn_ref, hc_ref, carry_ref,
                 *, n_chunks, cc, tiles_per_seq):
    tm = x_ref.shape[0]
    i = pl.program_id(0)

    @pl.when(i % tiles_per_seq == 0)
    def _():
        carry_ref[...] = jnp.zeros_like(carry_ref)

    xn_ref[...] = _rmsnorm(x_ref[...], g_ref[...]).astype(BF16)
    row = lax.broadcasted_iota(jnp.int32, (tm, cc), 0)
    d = n_chunks * cc
    for j in range(n_chunks):
        cols = slice(j * cc, (j + 1) * cc)
        gate_b, gate_c, u = (
            jnp.dot(xn_ref[...], win_ref[:, part * d + j * cc:part * d + (j + 1) * cc],
                    preferred_element_type=F32) for part in range(3))
        z = gate_c * u
        prev = carry_ref[j]
        carry_ref[j] = z[tm - 8:, :]
        p1 = jnp.broadcast_to(prev[7:8, :], (tm, cc))
        p2 = jnp.broadcast_to(prev[6:7, :], (tm, cc))
        z1 = jnp.where(row == 0, p1, pltpu.roll(z, 1, 0))
        z2 = jnp.where(row == 0, p2, jnp.where(row == 1, p1, pltpu.roll(z, 2, 0)))
        k = ck_ref[:, cols]
        conv = k[0:1, :] * z2
        conv = conv + k[1:2, :] * z1
        conv = conv + k[2:3, :] * z
        hc_ref[:, cols] = (gate_b * conv).astype(BF16)
    y = jnp.dot(hc_ref[...], wout_ref[...], preferred_element_type=F32)
    o_ref[...] = x_ref[...] + y


def _conv_mixer(x, gain, w_in, conv_k, w_out, layer, seq_len):
    n, d = x.shape
    tm = CONV_ROW_TILE
    cc = MXU_DIM
    n_chunks = d // cc
    assert seq_len % tm == 0 and n % seq_len == 0 and CONV_WIDTH - 1 <= 8
    return pl.pallas_call(
        functools.partial(_conv_kernel, n_chunks=n_chunks, cc=cc, tiles_per_seq=seq_len // tm),
        grid=(n // tm,),
        in_specs=[_row_spec(tm, d), _resident((1, d)), _resident_layer((d, 3 * d), layer),
                  _resident_layer((CONV_WIDTH, d), layer), _resident_layer((d, d), layer)],
        out_specs=_row_spec(tm, d),
        out_shape=jax.ShapeDtypeStruct((n, d), F32),
        scratch_shapes=[pltpu.VMEM((tm, d), BF16), pltpu.VMEM((tm, d), BF16),
                        pltpu.VMEM((n_chunks, 8, cc), F32)],
        compiler_params=_params("arbitrary"),
        name="conv_mixer",
    )(x, gain.reshape(1, d), w_in, conv_k, w_out)


def _rope_kernel(pos_ref, invf_ref, cos_ref, sin_ref):
    ang = pos_ref[...].astype(F32) * invf_ref[...]
    cos_ref[...] = jnp.cos(ang)
    sin_ref[...] = jnp.sin(ang)


def _rope_tables(positions_flat):
    n = positions_flat.shape[0]
    n_freq = HEAD_DIM // 2
    per_row = LANES // n_freq
    rows = n // per_row
    tm = min(ROW_TILE, rows)
    assert n % per_row == 0 and rows % tm == 0
    inv_freq = ROPE_THETA ** (-jnp.arange(0, HEAD_DIM, 2, dtype=F32) / HEAD_DIM)
    invf = jnp.tile(inv_freq, per_row).reshape(1, LANES)
    pos = jnp.repeat(positions_flat.reshape(rows, per_row), n_freq, axis=1)
    cos_d, sin_d = pl.pallas_call(
        _rope_kernel,
        grid=(rows // tm,),
        in_specs=[_row_spec(tm, LANES), _resident((1, LANES))],
        out_specs=[_row_spec(tm, LANES), _row_spec(tm, LANES)],
        out_shape=[jax.ShapeDtypeStruct((rows, LANES), F32)] * 2,
        compiler_params=_params("parallel"),
        name="rope_tables",
    )(pos, invf)
    cos_d = cos_d.reshape(n, n_freq)
    sin_d = sin_d.reshape(n, n_freq)
    reps = LANES // HEAD_DIM
    cos_t = jnp.tile(cos_d, (1, 2 * reps))
    sin_t = jnp.tile(jnp.concatenate([-sin_d, sin_d], axis=1), (1, reps))
    return cos_t, sin_t


def _transpose_blocks(t):
    return jnp.concatenate([t[r:r + WINDOW, :].T for r in range(0, t.shape[0], WINDOW)], axis=1)


def _attn_kernel(sinks_ref, x_ref, g_ref, cos_ref, sin_ref, wqkv_ref, wo_ref, o_ref,
                 xn_ref, q_ref, ka_ref, kb_ref, vat_ref, vbt_ref, att_ref,
                 *, tiles_per_seq, n_parts):
    tm, d = x_ref.shape
    part = tm // n_parts
    blocks_per_part = part // WINDOW
    n_q_slabs = d // LANES
    n_kv_slabs = N_KV_HEADS * HEAD_DIM // LANES
    i = pl.program_id(0)
    first = i % tiles_per_seq == 0

    @pl.when(first)
    def _():
        for r in (ka_ref, kb_ref):
            r[:, 0:WINDOW, :] = jnp.zeros((N_KV_HEADS, WINDOW, LANES), BF16)
        for r in (vat_ref, vbt_ref):
            r[:, :, 0:WINDOW] = jnp.zeros((N_KV_HEADS, LANES, WINDOW), BF16)

    @pl.when(jnp.logical_not(first))
    def _():
        for r in (ka_ref, kb_ref):
            r[:, 0:WINDOW, :] = r[:, tm:tm + WINDOW, :]
        for r in (vat_ref, vbt_ref):
            r[:, :, 0:WINDOW] = r[:, :, tm:tm + WINDOW]

    lane = lax.broadcasted_iota(jnp.int32, (part, LANES), 1)
    first_half = (lane & (HEAD_DIM // 2)) == 0
    low_head = lane < HEAD_DIM
    zero = jnp.zeros((part, LANES), F32)
    zero_t = jnp.zeros((HEAD_DIM, part), BF16)
    q_scale = LOG2_E / math.sqrt(HEAD_DIM)

    def rope(t, rows):
        partner = jnp.where(first_half,
                            pltpu.roll(t, LANES - HEAD_DIM // 2, 1),
                            pltpu.roll(t, HEAD_DIM // 2, 1))
        return t * cos_ref[rows, :] + partner * sin_ref[rows, :]

    def project_chunk(p, c):
        rows = slice(p * part, (p + 1) * part)
        halo_rows = slice(WINDOW + p * part, WINDOW + (p + 1) * part)
        if c == 0:
            xn_ref[rows, :] = _rmsnorm(x_ref[rows, :], g_ref[...]).astype(BF16)
        t2 = jnp.dot(xn_ref[rows, :], wqkv_ref[:, c * MXU_DIM:(c + 1) * MXU_DIM],
                     preferred_element_type=F32)
        for half in range(MXU_DIM // LANES):
            s = c * (MXU_DIM // LANES) + half
            t = t2[:, half * LANES:(half + 1) * LANES]
            if s < n_q_slabs:
                q_ref[rows, s * LANES:(s + 1) * LANES] = (rope(t, rows) * q_scale).astype(BF16)
            elif s < n_q_slabs + n_kv_slabs:
                ks = s - n_q_slabs
                k_slab = rope(t, rows)
                swapped = pltpu.roll(k_slab, HEAD_DIM, 1)
                ka_ref[2 * ks, halo_rows, :] = jnp.where(low_head, k_slab, zero).astype(BF16)
                kb_ref[2 * ks, halo_rows, :] = jnp.where(low_head, zero, swapped).astype(BF16)
                ka_ref[2 * ks + 1, halo_rows, :] = jnp.where(low_head, swapped, zero).astype(BF16)
                kb_ref[2 * ks + 1, halo_rows, :] = jnp.where(low_head, zero, k_slab).astype(BF16)
            else:
                vs = s - n_q_slabs - n_kv_slabs
                v_t = _transpose_blocks(t).astype(BF16)
                for head, v_head in ((2 * vs, v_t[:HEAD_DIM, :]), (2 * vs + 1, v_t[HEAD_DIM:, :])):
                    vat_ref[head, :, halo_rows] = jnp.concatenate([v_head, zero_t], axis=0)
                    vbt_ref[head, :, halo_rows] = jnp.concatenate([zero_t, v_head], axis=0)

    def out_chunk(p, c):
        rows = slice(p * part, (p + 1) * part)
        cols = slice(c * MXU_DIM, (c + 1) * MXU_DIM)
        y = jnp.dot(att_ref[rows, :], wo_ref[:, cols], preferred_element_type=F32)
        o_ref[rows, cols] = x_ref[rows, cols] + y

    project_chunks = wqkv_ref.shape[1] // MXU_DIM
    out_chunks = d // MXU_DIM

    kj = lax.broadcasted_iota(jnp.int32, (WINDOW, WINDOW), 0)
    qi = lax.broadcasted_iota(jnp.int32, (WINDOW, WINDOW), 1)
    cur_ok = kj <= qi
    prev_ok = kj > qi
    prev_ok_first = kj > qi + jnp.where(first, WINDOW, 0)
    neg = jnp.finfo(F32).min

    def scores(b, g):
        rows = slice(b * WINDOW, (b + 1) * WINDOW)
        keys = slice(b * WINDOW, b * WINDOW + 2 * WINDOW)
        q2 = jnp.concatenate([q_ref[rows, sl * LANES:(sl + 1) * LANES]
                              for sl in (2 * g, 2 * g + 1)], axis=0)
        k2 = jnp.concatenate([ka_ref[g, keys, :], kb_ref[g, keys, :]], axis=0)
        return lax.dot_general(k2, q2, (((1,), (1,)), ((), ())),
                               preferred_element_type=F32)

    def attend(b, g, s_t):
        rows = slice(b * WINDOW, (b + 1) * WINDOW)
        keys = slice(b * WINDOW, b * WINDOW + 2 * WINDOW)
        mask = jnp.concatenate([prev_ok_first if b == 0 else prev_ok, cur_ok], axis=0)
        p_cols = []
        inv = []
        for pair in range(2):
            p_col = []
            inv_col = []
            for lo in range(2):
                sink = sinks_ref[GROUP * g + 2 * pair + lo] * LOG2_E
                sc = s_t[lo * 2 * WINDOW:(lo + 1) * 2 * WINDOW, pair * WINDOW:(pair + 1) * WINDOW]
                sc = jnp.where(mask, sc, neg)
                m = jnp.maximum(jnp.max(sc, axis=0, keepdims=True), sink)
                p = jnp.exp2(sc - m)
                denom = jnp.sum(p, axis=0, keepdims=True) + jnp.exp2(sink - m)
                p_col.append(p.astype(BF16))
                inv_col.append(1.0 / denom)
            p_cols.append(jnp.concatenate(p_col, axis=0))
            inv.append(inv_col)
        p_t = jnp.concatenate(p_cols, axis=1)
        v2_t = jnp.concatenate([vat_ref[g, :, keys], vbt_ref[g, :, keys]], axis=1)
        o_t = jnp.dot(v2_t, p_t, preferred_element_type=F32)
        for pair in range(2):
            cols = slice(pair * WINDOW, (pair + 1) * WINDOW)
            slab = 2 * g + pair
            o_pair = jnp.concatenate([o_t[:HEAD_DIM, cols] * inv[pair][0],
                                      o_t[HEAD_DIM:, cols] * inv[pair][1]], axis=0)
            att_ref[rows, slab * LANES:(slab + 1) * LANES] = o_pair.T.astype(BF16)

    units = [(b, g) for b in range(tm // WINDOW) for g in range(N_KV_HEADS)]
    units_per_part = blocks_per_part * N_KV_HEADS
    filler_slots = units_per_part - SCORE_LOOKAHEAD

    for c in range(project_chunks):
        project_chunk(0, c)
    pending = [scores(*u) for u in units[:SCORE_LOOKAHEAD]]
    fillers = []
    for n, (b, g) in enumerate(units):
        p, j = divmod(n, units_per_part)
        if j == 0:
            fillers = []
            if p + 1 < n_parts:
                fillers += [functools.partial(project_chunk, p + 1, c)
                            for c in range(project_chunks)]
            if p >= 1:
                fillers += [functools.partial(out_chunk, p - 1, c) for c in range(out_chunks)]
        if n + SCORE_LOOKAHEAD < len(units):
            pending.append(scores(*units[n + SCORE_LOOKAHEAD]))
        attend(b, g, pending[n])
        pending[n] = None
        if j < filler_slots:
            lo_f = len(fillers) * j // filler_slots
            hi_f = len(fillers) * (j + 1) // filler_slots
            for filler in fillers[lo_f:hi_f]:
                filler()
    for c in range(out_chunks):
        out_chunk(n_parts - 1, c)


def _attention(x, gain, cos_t, sin_t, w_qkv, sinks, w_o, layer, seq_len):
    n, d = x.shape
    tm = ATTN_ROW_TILE
    n_parts = ATTN_PARTS
    qkv_w = w_qkv.shape[2]
    assert seq_len % tm == 0 and tm % (n_parts * WINDOW) == 0 and d == N_Q_HEADS * HEAD_DIM
    assert qkv_w % MXU_DIM == 0 and (tm // n_parts // WINDOW) * N_KV_HEADS > SCORE_LOOKAHEAD
    k_halo = pltpu.VMEM((N_KV_HEADS, tm + WINDOW, LANES), BF16)
    vt_halo = pltpu.VMEM((N_KV_HEADS, LANES, tm + WINDOW), BF16)
    return pl.pallas_call(
        functools.partial(_attn_kernel, tiles_per_seq=seq_len // tm, n_parts=n_parts),
        grid=(n // tm,),
        in_specs=[pl.BlockSpec(memory_space=pltpu.SMEM),
                  _row_spec(tm, d), _resident((1, d)), _row_spec(tm, LANES), _row_spec(tm, LANES),
                  _resident_layer((d, qkv_w), layer), _resident_layer((d, d), layer)],
        out_specs=_row_spec(tm, d),
        out_shape=jax.ShapeDtypeStruct((n, d), F32),
        scratch_shapes=[pltpu.VMEM((tm, d), BF16), pltpu.VMEM((tm, d), BF16),
                        k_halo, k_halo, vt_halo, vt_halo,
                        pltpu.VMEM((tm, d), BF16)],
        compiler_params=_params("arbitrary"),
        name="swa_attention",
    )(sinks, x, gain.reshape(1, d), cos_t, sin_t, w_qkv, w_o)


def kernel(x, positions, ln_ff1, w_ff1_in, w_ff1_out, ln_mix, ln_ff2, w_ff2_in, w_ff2_out,
           conv_w_in, conv_kernel, conv_w_out, attn_w_qkv, attn_sinks, attn_w_o, ln_final):
    batch, seq_len, d = x.shape
    depth = ln_ff1.shape[0]
    h = x.reshape(batch * seq_len, d)
    cos_t, sin_t = _rope_tables(positions.reshape(batch * seq_len))
    w_ff1_in, w_ff1_out, w_ff2_in, w_ff2_out, conv_w_in, conv_w_out, attn_w_qkv, attn_w_o = (
        w.astype(BF16) for w in (w_ff1_in, w_ff1_out, w_ff2_in, w_ff2_out,
                                 conv_w_in, conv_w_out, attn_w_qkv, attn_w_o))
    for i in range(depth):
        h = _ffn(h, ln_ff1[i], w_ff1_in, w_ff1_out, i)
        j = i // 2
        if i % 2 == 0:
            h = _conv_mixer(h, ln_mix[i], conv_w_in, conv_kernel, conv_w_out, j, seq_len)
        else:
            h = _attention(h, ln_mix[i], cos_t, sin_t, attn_w_qkv, attn_sinks[j], attn_w_o,
                           j, seq_len)
        h = _ffn(h, ln_ff2[i], w_ff2_in, w_ff2_out, i,
                 final_gain=ln_final if i == depth - 1 else None)
    return h.reshape(batch, seq_len, d)
```

```python
import functools
import math

import jax
import jax.numpy as jnp
from jax import lax
from jax.experimental import pallas as pl
from jax.experimental.pallas import tpu as pltpu

RMS_EPS = 1e-6
ROPE_THETA = 10000.0
HEAD_DIM = 64
N_Q_HEADS = 16
N_KV_HEADS = 4
GROUP = N_Q_HEADS // N_KV_HEADS
WINDOW = 128
CONV_WIDTH = 3
LANES = 128
BF16_SUBLANES = 16
MXU_DIM = 256

ROW_TILE = 512
CONV_ROW_TILE = 1024
FFN_ROW_TILE = 1024
ATTN_ROW_TILE = 1024
ATTN_PARTS = 4
SCORE_LOOKAHEAD = 2
LOG2_E = math.log2(math.e)
VMEM_LIMIT_BYTES = 56 * 1024 * 1024

F32 = jnp.float32
BF16 = jnp.bfloat16


def _rmsnorm(x, g):
    ms = jnp.mean(x * x, axis=-1, keepdims=True)
    return (x * lax.rsqrt(ms + RMS_EPS)) * g


def _resident(shape):
    return pl.BlockSpec(shape, lambda i: (0,) * len(shape), pipeline_mode=pl.Buffered(1))


def _resident_layer(shape, layer):
    return pl.BlockSpec((None,) + shape, lambda i: (layer,) + (0,) * len(shape),
                        pipeline_mode=pl.Buffered(1))


def _row_spec(tm, width):
    return pl.BlockSpec((tm, width), lambda i: (i, 0))


def _params():
    return pltpu.CompilerParams(dimension_semantics=("arbitrary",),
                                vmem_limit_bytes=VMEM_LIMIT_BYTES)


def _cast_plan(weights, n_steps):
    in_specs, out_specs, out_shapes = [], [], []
    for w, layer in weights:
        _, rows, cols = w.shape
        block_rows = next(r for r in range(BF16_SUBLANES, rows + 1, BF16_SUBLANES)
                          if rows % r == 0 and rows // r <= n_steps)
        last = rows // block_rows - 1
        in_specs.append(pl.BlockSpec((None, block_rows, cols),
                                     lambda i, layer=layer, last=last: (layer, jnp.minimum(i, last), 0)))
        out_specs.append(pl.BlockSpec((block_rows, cols),
                                      lambda i, last=last: (jnp.minimum(i, last), 0)))
        out_shapes.append(jax.ShapeDtypeStruct((rows, cols), BF16))
    return in_specs, out_specs, out_shapes


def _cast_blocks(src_refs, dst_refs):
    for src, dst in zip(src_refs, dst_refs):
        dst[...] = src[...].astype(BF16)


def _ffn_kernel(*refs, n_chunks, fc, final_norm, n_cast):
    n_in = 5 if final_norm else 4
    x_ref, g_ref, win_ref, wo_ref = refs[:4]
    gf_ref = refs[4] if final_norm else None
    cast_src = refs[n_in:n_in + n_cast]
    o_ref = refs[n_in + n_cast]
    cast_dst = refs[n_in + n_cast + 1:n_in + 2 * n_cast + 1]
    xn_ref, h_ref = refs[n_in + 2 * n_cast + 1:]
    f = n_chunks * fc
    _cast_blocks(cast_src, cast_dst)
    xn_ref[...] = _rmsnorm(x_ref[...], g_ref[...]).astype(BF16)
    for j in range(n_chunks):
        gate = jnp.dot(xn_ref[...], win_ref[:, j * fc:(j + 1) * fc], preferred_element_type=F32)
        up = jnp.dot(xn_ref[...], win_ref[:, f + j * fc:f + (j + 1) * fc],
                     preferred_element_type=F32)
        h_ref[:, j * fc:(j + 1) * fc] = ((gate / (1.0 + jnp.exp(-gate))) * up).astype(BF16)
    y = jnp.dot(h_ref[...], wo_ref[...], preferred_element_type=F32)
    out = x_ref[...] + 0.5 * y
    if final_norm:
        out = _rmsnorm(out, gf_ref[...])
    o_ref[...] = out


def _ffn(x, gain, w_in, w_out, cast_next, final_gain=None):
    n, d = x.shape
    f = w_out.shape[0]
    fc = MXU_DIM
    n_chunks = f // fc
    tm = FFN_ROW_TILE
    assert n_chunks * fc == f and n % tm == 0
    final_norm = final_gain is not None
    in_specs = [_row_spec(tm, d), _resident((1, d)), _resident((d, 2 * f)), _resident((f, d))]
    args = [x, gain.reshape(1, d), w_in, w_out]
    if final_norm:
        in_specs.append(_resident((1, d)))
        args.append(final_gain.reshape(1, d))
    cast_in, cast_out, cast_shapes = _cast_plan(cast_next, n // tm)
    out, *casts = pl.pallas_call(
        functools.partial(_ffn_kernel, n_chunks=n_chunks, fc=fc, final_norm=final_norm,
                          n_cast=len(cast_next)),
        grid=(n // tm,),
        in_specs=in_specs + cast_in,
        out_specs=[_row_spec(tm, d)] + cast_out,
        out_shape=[jax.ShapeDtypeStruct((n, d), F32)] + cast_shapes,
        scratch_shapes=[pltpu.VMEM((tm, d), BF16), pltpu.VMEM((tm, f), BF16)],
        compiler_params=_params(),
        name="ffn_final" if final_norm else "ffn",
    )(*args, *(w for w, _ in cast_next))
    return out, casts


def _conv_kernel(*refs, n_chunks, cc, tiles_per_seq, n_cast):
    x_ref, g_ref, win_ref, ck_ref, wout_ref = refs[:5]
    cast_src = refs[5:5 + n_cast]
    o_ref = refs[5 + n_cast]
    cast_dst = refs[6 + n_cast:6 + 2 * n_cast]
    xn_ref, hc_ref, carry_ref = refs[6 + 2 * n_cast:]
    tm = x_ref.shape[0]
    i = pl.program_id(0)
    _cast_blocks(cast_src, cast_dst)

    @pl.when(i % tiles_per_seq == 0)
    def _():
        carry_ref[...] = jnp.zeros_like(carry_ref)

    xn_ref[...] = _rmsnorm(x_ref[...], g_ref[...]).astype(BF16)
    row = lax.broadcasted_iota(jnp.int32, (tm, cc), 0)
    d = n_chunks * cc
    for j in range(n_chunks):
        cols = slice(j * cc, (j + 1) * cc)
        gate_b, gate_c, u = (
            jnp.dot(xn_ref[...], win_ref[:, part * d + j * cc:part * d + (j + 1) * cc],
                    preferred_element_type=F32) for part in range(3))
        z = gate_c * u
        prev = carry_ref[j]
        carry_ref[j] = z[tm - 8:, :]
        p1 = jnp.broadcast_to(prev[7:8, :], (tm, cc))
        p2 = jnp.broadcast_to(prev[6:7, :], (tm, cc))
        z1 = jnp.where(row == 0, p1, pltpu.roll(z, 1, 0))
        z2 = jnp.where(row == 0, p2, jnp.where(row == 1, p1, pltpu.roll(z, 2, 0)))
        k = ck_ref[:, cols]
        conv = k[0:1, :] * z2
        conv = conv + k[1:2, :] * z1
        conv = conv + k[2:3, :] * z
        hc_ref[:, cols] = (gate_b * conv).astype(BF16)
    y = jnp.dot(hc_ref[...], wout_ref[...], preferred_element_type=F32)
    o_ref[...] = x_ref[...] + y


def _conv_mixer(x, gain, w_in, conv_k, w_out, layer, seq_len, cast_next):
    n, d = x.shape
    tm = CONV_ROW_TILE
    cc = MXU_DIM
    n_chunks = d // cc
    assert seq_len % tm == 0 and n % seq_len == 0 and CONV_WIDTH - 1 <= 8
    cast_in, cast_out, cast_shapes = _cast_plan(cast_next, n // tm)
    out, *casts = pl.pallas_call(
        functools.partial(_conv_kernel, n_chunks=n_chunks, cc=cc, tiles_per_seq=seq_len // tm,
                          n_cast=len(cast_next)),
        grid=(n // tm,),
        in_specs=[_row_spec(tm, d), _resident((1, d)), _resident((d, 3 * d)),
                  _resident_layer((CONV_WIDTH, d), layer), _resident((d, d))] + cast_in,
        out_specs=[_row_spec(tm, d)] + cast_out,
        out_shape=[jax.ShapeDtypeStruct((n, d), F32)] + cast_shapes,
        scratch_shapes=[pltpu.VMEM((tm, d), BF16), pltpu.VMEM((tm, d), BF16),
                        pltpu.VMEM((n_chunks, 8, cc), F32)],
        compiler_params=_params(),
        name="conv_mixer",
    )(x, gain.reshape(1, d), w_in, conv_k, w_out, *(w for w, _ in cast_next))
    return out, casts


def _rope_kernel(pos_ref, invf_ref, cos_ref, sin_ref):
    n_freq = HEAD_DIM // 2
    ang = pos_ref[...].astype(F32) * invf_ref[...]
    lane = lax.broadcasted_iota(jnp.int32, ang.shape, 1)
    sign = jnp.where((lane & n_freq) == 0, -1.0, 1.0)
    for dense, out_ref, factor in ((jnp.cos(ang), cos_ref, None), (jnp.sin(ang), sin_ref, sign)):
        for q in range(LANES // n_freq):
            t = dense if q == 0 else pltpu.roll(dense, LANES - q * n_freq, 1)
            t = jnp.where(lane < n_freq, t, pltpu.roll(t, n_freq, 1))
            t = jnp.where(lane < 2 * n_freq, t, pltpu.roll(t, 2 * n_freq, 1))
            out_ref[q] = t if factor is None else t * factor


def _rope_tables(positions_flat):
    n = positions_flat.shape[0]
    n_freq = HEAD_DIM // 2
    groups = LANES // n_freq
    rows = n // groups
    tm = min(ROW_TILE, rows)
    assert n % groups == 0 and rows % tm == 0
    inv_freq = ROPE_THETA ** (-jnp.arange(0, HEAD_DIM, 2, dtype=F32) / HEAD_DIM)
    invf = jnp.tile(inv_freq, groups).reshape(1, LANES)
    pos = jnp.repeat(positions_flat.reshape(groups, rows).T, n_freq, axis=1)
    out_spec = pl.BlockSpec((groups, tm, LANES), lambda i: (0, i, 0))
    cos_t, sin_t = pl.pallas_call(
        _rope_kernel,
        grid=(rows // tm,),
        in_specs=[_row_spec(tm, LANES), _resident((1, LANES))],
        out_specs=[out_spec, out_spec],
        out_shape=[jax.ShapeDtypeStruct((groups, rows, LANES), F32)] * 2,
        compiler_params=_params(),
        name="rope_tables",
    )(pos, invf)
    return cos_t.reshape(n, LANES), sin_t.reshape(n, LANES)


def _transpose_blocks(t):
    return jnp.concatenate([t[r:r + WINDOW, :].T for r in range(0, t.shape[0], WINDOW)], axis=1)


def _attn_kernel(*refs, tiles_per_seq, n_parts, n_cast):
    sinks_ref, x_ref, g_ref, cos_ref, sin_ref, wqkv_ref, wo_ref = refs[:7]
    cast_src = refs[7:7 + n_cast]
    o_ref = refs[7 + n_cast]
    cast_dst = refs[8 + n_cast:8 + 2 * n_cast]
    xn_ref, q_ref, ka_ref, kb_ref, vat_ref, vbt_ref, att_ref = refs[8 + 2 * n_cast:]
    _cast_blocks(cast_src, cast_dst)
    tm, d = x_ref.shape
    part = tm // n_parts
    blocks_per_part = part // WINDOW
    n_q_slabs = d // LANES
    n_kv_slabs = N_KV_HEADS * HEAD_DIM // LANES
    i = pl.program_id(0)
    first = i % tiles_per_seq == 0

    @pl.when(first)
    def _():
        for r in (ka_ref, kb_ref):
            r[:, 0:WINDOW, :] = jnp.zeros((N_KV_HEADS, WINDOW, LANES), BF16)
        for r in (vat_ref, vbt_ref):
            r[:, :, 0:WINDOW] = jnp.zeros((N_KV_HEADS, LANES, WINDOW), BF16)

    @pl.when(jnp.logical_not(first))
    def _():
        for r in (ka_ref, kb_ref):
            r[:, 0:WINDOW, :] = r[:, tm:tm + WINDOW, :]
        for r in (vat_ref, vbt_ref):
            r[:, :, 0:WINDOW] = r[:, :, tm:tm + WINDOW]

    lane = lax.broadcasted_iota(jnp.int32, (part, LANES), 1)
    first_half = (lane & (HEAD_DIM // 2)) == 0
    low_head = lane < HEAD_DIM
    zero = jnp.zeros((part, LANES), F32)
    zero_t = jnp.zeros((HEAD_DIM, part), BF16)
    q_scale = LOG2_E / math.sqrt(HEAD_DIM)

    def rope(t, rows):
        partner = jnp.where(first_half,
                            pltpu.roll(t, LANES - HEAD_DIM // 2, 1),
                            pltpu.roll(t, HEAD_DIM // 2, 1))
        return t * cos_ref[rows, :] + partner * sin_ref[rows, :]

    def project_chunk(p, c):
        rows = slice(p * part, (p + 1) * part)
        halo_rows = slice(WINDOW + p * part, WINDOW + (p + 1) * part)
        if c == 0:
            xn_ref[rows, :] = _rmsnorm(x_ref[rows, :], g_ref[...]).astype(BF16)
        t2 = jnp.dot(xn_ref[rows, :], wqkv_ref[:, c * MXU_DIM:(c + 1) * MXU_DIM],
                     preferred_element_type=F32)
        for half in range(MXU_DIM // LANES):
            s = c * (MXU_DIM // LANES) + half
            t = t2[:, half * LANES:(half + 1) * LANES]
            if s < n_q_slabs:
                q_ref[rows, s * LANES:(s + 1) * LANES] = (rope(t, rows) * q_scale).astype(BF16)
            elif s < n_q_slabs + n_kv_slabs:
                ks = s - n_q_slabs
                k_slab = rope(t, rows)
                swapped = pltpu.roll(k_slab, HEAD_DIM, 1)
                ka_ref[2 * ks, halo_rows, :] = jnp.where(low_head, k_slab, zero).astype(BF16)
                kb_ref[2 * ks, halo_rows, :] = jnp.where(low_head, zero, swapped).astype(BF16)
                ka_ref[2 * ks + 1, halo_rows, :] = jnp.where(low_head, swapped, zero).astype(BF16)
                kb_ref[2 * ks + 1, halo_rows, :] = jnp.where(low_head, zero, k_slab).astype(BF16)
            else:
                vs = s - n_q_slabs - n_kv_slabs
                v_t = _transpose_blocks(t).astype(BF16)
                for head, v_head in ((2 * vs, v_t[:HEAD_DIM, :]), (2 * vs + 1, v_t[HEAD_DIM:, :])):
                    vat_ref[head, :, halo_rows] = jnp.concatenate([v_head, zero_t], axis=0)
                    vbt_ref[head, :, halo_rows] = jnp.concatenate([zero_t, v_head], axis=0)

    def out_chunk(p, c):
        rows = slice(p * part, (p + 1) * part)
        cols = slice(c * MXU_DIM, (c + 1) * MXU_DIM)
        y = jnp.dot(att_ref[rows, :], wo_ref[:, cols], preferred_element_type=F32)
        o_ref[rows, cols] = x_ref[rows, cols] + y

    project_chunks = wqkv_ref.shape[1] // MXU_DIM
    out_chunks = d // MXU_DIM

    kj = lax.broadcasted_iota(jnp.int32, (WINDOW, WINDOW), 0)
    qi = lax.broadcasted_iota(jnp.int32, (WINDOW, WINDOW), 1)
    cur_ok = kj <= qi
    prev_ok = kj > qi
    prev_ok_first = kj > qi + jnp.where(first, WINDOW, 0)
    neg = jnp.finfo(F32).min

    def scores(b, g):
        rows = slice(b * WINDOW, (b + 1) * WINDOW)
        keys = slice(b * WINDOW, b * WINDOW + 2 * WINDOW)
        q2 = jnp.concatenate([q_ref[rows, sl * LANES:(sl + 1) * LANES]
                              for sl in (2 * g, 2 * g + 1)], axis=0)
        k2 = jnp.concatenate([ka_ref[g, keys, :], kb_ref[g, keys, :]], axis=0)
        return lax.dot_general(k2, q2, (((1,), (1,)), ((), ())),
                               preferred_element_type=F32)

    def attend(b, g, s_t):
        rows = slice(b * WINDOW, (b + 1) * WINDOW)
        keys = slice(b * WINDOW, b * WINDOW + 2 * WINDOW)
        mask = jnp.concatenate([prev_ok_first if b == 0 else prev_ok, cur_ok], axis=0)
        p_cols = []
        inv = []
        for pair in range(2):
            p_col = []
            inv_col = []
            for lo in range(2):
                sink = sinks_ref[GROUP * g + 2 * pair + lo] * LOG2_E
                sc = s_t[lo * 2 * WINDOW:(lo + 1) * 2 * WINDOW, pair * WINDOW:(pair + 1) * WINDOW]
                sc = jnp.where(mask, sc, neg)
                m = jnp.maximum(jnp.max(sc, axis=0, keepdims=True), sink)
                p = jnp.exp2(sc - m)
                denom = jnp.sum(p, axis=0, keepdims=True) + jnp.exp2(sink - m)
                p_col.append(p.astype(BF16))
                inv_col.append(1.0 / denom)
            p_cols.append(jnp.concatenate(p_col, axis=0))
            inv.append(inv_col)
        p_t = jnp.concatenate(p_cols, axis=1)
        v2_t = jnp.concatenate([vat_ref[g, :, keys], vbt_ref[g, :, keys]], axis=1)
        o_t = jnp.dot(v2_t, p_t, preferred_element_type=F32)
        for pair in range(2):
            cols = slice(pair * WINDOW, (pair + 1) * WINDOW)
            slab = 2 * g + pair
            o_pair = jnp.concatenate([o_t[:HEAD_DIM, cols] * inv[pair][0],
                                      o_t[HEAD_DIM:, cols] * inv[pair][1]], axis=0)
            att_ref[rows, slab * LANES:(slab + 1) * LANES] = o_pair.T.astype(BF16)

    units = [(b, g) for b in range(tm // WINDOW) for g in range(N_KV_HEADS)]
    units_per_part = blocks_per_part * N_KV_HEADS
    filler_slots = units_per_part - SCORE_LOOKAHEAD

    for c in range(project_chunks):
        project_chunk(0, c)
    pending = [scores(*u) for u in units[:SCORE_LOOKAHEAD]]
    fillers = []
    for n, (b, g) in enumerate(units):
        p, j = divmod(n, units_per_part)
        if j == 0:
            fillers = []
            if p + 1 < n_parts:
                fillers += [functools.partial(project_chunk, p + 1, c)
                            for c in range(project_chunks)]
            if p >= 1:
                fillers += [functools.partial(out_chunk, p - 1, c) for c in range(out_chunks)]
        if n + SCORE_LOOKAHEAD < len(units):
            pending.append(scores(*units[n + SCORE_LOOKAHEAD]))
        attend(b, g, pending[n])
        pending[n] = None
        if j < filler_slots:
            lo_f = len(fillers) * j // filler_slots
            hi_f = len(fillers) * (j + 1) // filler_slots
            for filler in fillers[lo_f:hi_f]:
                filler()
    for c in range(out_chunks):
        out_chunk(n_parts - 1, c)


def _attention(x, gain, cos_t, sin_t, w_qkv, sinks, w_o, seq_len, cast_next):
    n, d = x.shape
    tm = ATTN_ROW_TILE
    n_parts = ATTN_PARTS
    qkv_w = w_qkv.shape[1]
    assert seq_len % tm == 0 and tm % (n_parts * WINDOW) == 0 and d == N_Q_HEADS * HEAD_DIM
    assert qkv_w % MXU_DIM == 0 and (tm // n_parts // WINDOW) * N_KV_HEADS > SCORE_LOOKAHEAD
    k_halo = pltpu.VMEM((N_KV_HEADS, tm + WINDOW, LANES), BF16)
    vt_halo = pltpu.VMEM((N_KV_HEADS, LANES, tm + WINDOW), BF16)
    cast_in, cast_out, cast_shapes = _cast_plan(cast_next, n // tm)
    out, *casts = pl.pallas_call(
        functools.partial(_attn_kernel, tiles_per_seq=seq_len // tm, n_parts=n_parts,
                          n_cast=len(cast_next)),
        grid=(n // tm,),
        in_specs=[pl.BlockSpec(memory_space=pltpu.SMEM),
                  _row_spec(tm, d), _resident((1, d)), _row_spec(tm, LANES), _row_spec(tm, LANES),
                  _resident((d, qkv_w)), _resident((d, d))] + cast_in,
        out_specs=[_row_spec(tm, d)] + cast_out,
        out_shape=[jax.ShapeDtypeStruct((n, d), F32)] + cast_shapes,
        scratch_shapes=[pltpu.VMEM((tm, d), BF16), pltpu.VMEM((tm, d), BF16),
                        k_halo, k_halo, vt_halo, vt_halo,
                        pltpu.VMEM((tm, d), BF16)],
        compiler_params=_params(),
        name="swa_attention",
    )(sinks, x, gain.reshape(1, d), cos_t, sin_t, w_qkv, w_o, *(w for w, _ in cast_next))
    return out, casts


def kernel(x, positions, ln_ff1, w_ff1_in, w_ff1_out, ln_mix, ln_ff2, w_ff2_in, w_ff2_out,
           conv_w_in, conv_kernel, conv_w_out, attn_w_qkv, attn_sinks, attn_w_o, ln_final):
    batch, seq_len, d = x.shape
    depth = ln_ff1.shape[0]
    h = x.reshape(batch * seq_len, d)
    cos_t, sin_t = _rope_tables(positions.reshape(batch * seq_len))

    stages = []
    for i in range(depth):
        j = i // 2
        stages.append(("ffn", [(w_ff1_in, i), (w_ff1_out, i)],
                       dict(gain=ln_ff1[i], final_gain=None)))
        if i % 2 == 0:
            stages.append(("conv", [(conv_w_in, j), (conv_w_out, j)],
                           dict(gain=ln_mix[i], layer=j)))
        else:
            stages.append(("attn", [(attn_w_qkv, j), (attn_w_o, j)],
                           dict(gain=ln_mix[i], sinks=attn_sinks[j])))
        stages.append(("ffn", [(w_ff2_in, i), (w_ff2_out, i)],
                       dict(gain=ln_ff2[i], final_gain=ln_final if i == depth - 1 else None)))

    weights = [w[layer].astype(BF16) for w, layer in stages[0][1]]
    for k, (kind, _, args) in enumerate(stages):
        cast_next = stages[k + 1][1] if k + 1 < len(stages) else []
        if kind == "ffn":
            h, weights = _ffn(h, args["gain"], *weights, cast_next,
                              final_gain=args["final_gain"])
        elif kind == "conv":
            h, weights = _conv_mixer(h, args["gain"], weights[0], conv_kernel, weights[1],
                                     args["layer"], seq_len, cast_next)
        else:
            h, weights = _attention(h, args["gain"], cos_t, sin_t, weights[0], args["sinks"],
                                    weights[1], seq_len, cast_next)
    return h.reshape(batch, seq_len, d)
```

```python
import functools
import math

import jax
import jax.numpy as jnp
from jax import lax
from jax.experimental import pallas as pl
from jax.experimental.pallas import tpu as pltpu

RMS_EPS = 1e-6
ROPE_THETA = 10000.0
HEAD_DIM = 64
N_Q_HEADS = 16
N_KV_HEADS = 4
GROUP = N_Q_HEADS // N_KV_HEADS
WINDOW = 128
CONV_WIDTH = 3
LANES = 128
BF16_SUBLANES = 16
MXU_DIM = 256

ROW_TILE = 512
CONV_ROW_TILE = 1024
FFN_ROW_TILE = 1024
FFN_CHUNK = 256
NORM_PARTS = 4
ATTN_ROW_TILE = 1024
ATTN_PARTS = 4
SCORE_LOOKAHEAD = 2
LOG2_E = math.log2(math.e)
VMEM_LIMIT_BYTES = 56 * 1024 * 1024

F32 = jnp.float32
BF16 = jnp.bfloat16


def _rmsnorm(x, g):
    ms = jnp.mean(x * x, axis=-1, keepdims=True)
    return (x * lax.rsqrt(ms + RMS_EPS)) * g


def _resident(shape):
    return pl.BlockSpec(shape, lambda i: (0,) * len(shape), pipeline_mode=pl.Buffered(1))


def _resident_layer(shape, layer):
    return pl.BlockSpec((None,) + shape, lambda i: (layer,) + (0,) * len(shape),
                        pipeline_mode=pl.Buffered(1))


def _row_spec(tm, width):
    return pl.BlockSpec((tm, width), lambda i: (i, 0))


def _params():
    return pltpu.CompilerParams(dimension_semantics=("arbitrary",),
                                vmem_limit_bytes=VMEM_LIMIT_BYTES)


def _cast_plan(weights, n_steps):
    in_specs, out_specs, out_shapes = [], [], []
    for w, layer in weights:
        _, rows, cols = w.shape
        block_rows = next(r for r in range(BF16_SUBLANES, rows + 1, BF16_SUBLANES)
                          if rows % r == 0 and rows // r <= n_steps)
        last = rows // block_rows - 1
        in_specs.append(pl.BlockSpec((None, block_rows, cols),
                                     lambda i, layer=layer, last=last: (layer, jnp.minimum(i, last), 0)))
        out_specs.append(pl.BlockSpec((block_rows, cols),
                                      lambda i, last=last: (jnp.minimum(i, last), 0)))
        out_shapes.append(jax.ShapeDtypeStruct((rows, cols), BF16))
    return in_specs, out_specs, out_shapes


def _cast_blocks(src_refs, dst_refs):
    for src, dst in zip(src_refs, dst_refs):
        dst[...] = src[...].astype(BF16)


def _ffn_kernel(*refs, fc, final_norm, n_cast):
    n_in = 5 if final_norm else 4
    x_ref, g_ref, win_ref, wo_ref = refs[:4]
    gf_ref = refs[4] if final_norm else None
    cast_src = refs[n_in:n_in + n_cast]
    o_ref = refs[n_in + n_cast]
    cast_dst = refs[n_in + n_cast + 1:n_in + 2 * n_cast + 1]
    xn_ref, h_ref = refs[n_in + 2 * n_cast + 1:]
    tm = x_ref.shape[0]
    f = wo_ref.shape[0]

    def hidden(rows, lo, hi):
        gate = jnp.dot(xn_ref[rows, :], win_ref[:, lo:hi], preferred_element_type=F32)
        up = jnp.dot(xn_ref[rows, :], win_ref[:, f + lo:f + hi], preferred_element_type=F32)
        h_ref[rows, lo:hi] = ((gate / (1.0 + jnp.exp(-gate))) * up).astype(BF16)

    part = tm // NORM_PARTS
    for r in range(0, tm, part):
        rows = slice(r, r + part)
        xn_ref[rows, :] = _rmsnorm(x_ref[rows, :], g_ref[...]).astype(BF16)
        hidden(rows, 0, fc)
    for lo in range(fc, f, fc):
        hidden(slice(None), lo, min(lo + fc, f))
    y = jnp.dot(h_ref[...], wo_ref[...], preferred_element_type=F32)
    out = x_ref[...] + 0.5 * y
    if final_norm:
        out = _rmsnorm(out, gf_ref[...])
    o_ref[...] = out
    _cast_blocks(cast_src, cast_dst)


def _ffn(x, gain, w_in, w_out, cast_next, final_gain=None):
    n, d = x.shape
    f = w_out.shape[0]
    fc = FFN_CHUNK
    tm = FFN_ROW_TILE
    assert f % MXU_DIM == 0 and fc % MXU_DIM == 0 and n % tm == 0
    final_norm = final_gain is not None
    in_specs = [_row_spec(tm, d), _resident((1, d)), _resident((d, 2 * f)), _resident((f, d))]
    args = [x, gain.reshape(1, d), w_in, w_out]
    if final_norm:
        in_specs.append(_resident((1, d)))
        args.append(final_gain.reshape(1, d))
    cast_in, cast_out, cast_shapes = _cast_plan(cast_next, n // tm)
    out, *casts = pl.pallas_call(
        functools.partial(_ffn_kernel, fc=fc, final_norm=final_norm, n_cast=len(cast_next)),
        grid=(n // tm,),
        in_specs=in_specs + cast_in,
        out_specs=[_row_spec(tm, d)] + cast_out,
        out_shape=[jax.ShapeDtypeStruct((n, d), F32)] + cast_shapes,
        scratch_shapes=[pltpu.VMEM((tm, d), BF16), pltpu.VMEM((tm, f), BF16)],
        compiler_params=_params(),
        name="ffn_final" if final_norm else "ffn",
    )(*args, *(w for w, _ in cast_next))
    return out, casts


def _conv_kernel(*refs, n_chunks, cc, tiles_per_seq, n_cast):
    x_ref, g_ref, win_ref, ck_ref, wout_ref = refs[:5]
    cast_src = refs[5:5 + n_cast]
    o_ref = refs[5 + n_cast]
    cast_dst = refs[6 + n_cast:6 + 2 * n_cast]
    xn_ref, hc_ref, carry_ref = refs[6 + 2 * n_cast:]
    tm = x_ref.shape[0]
    i = pl.program_id(0)

    @pl.when(i % tiles_per_seq == 0)
    def _():
        carry_ref[...] = jnp.zeros_like(carry_ref)

    d = n_chunks * cc

    def mix(r0, nr, j):
        rows = slice(r0, r0 + nr)
        cols = slice(j * cc, (j + 1) * cc)
        gate_b, gate_c, u = (
            jnp.dot(xn_ref[rows, :], win_ref[:, part * d + j * cc:part * d + (j + 1) * cc],
                    preferred_element_type=F32) for part in range(3))
        z = gate_c * u
        prev = carry_ref[j]
        carry_ref[j] = z[nr - 8:, :]
        row = lax.broadcasted_iota(jnp.int32, (nr, cc), 0)
        p1 = jnp.broadcast_to(prev[7:8, :], (nr, cc))
        p2 = jnp.broadcast_to(prev[6:7, :], (nr, cc))
        z1 = jnp.where(row == 0, p1, pltpu.roll(z, 1, 0))
        z2 = jnp.where(row == 0, p2, jnp.where(row == 1, p1, pltpu.roll(z, 2, 0)))
        k = ck_ref[:, cols]
        conv = k[0:1, :] * z2
        conv = conv + k[1:2, :] * z1
        conv = conv + k[2:3, :] * z
        hc_ref[rows, cols] = (gate_b * conv).astype(BF16)

    xn_ref[...] = _rmsnorm(x_ref[...], g_ref[...]).astype(BF16)
    for j in range(n_chunks):
        mix(0, tm, j)
    y = jnp.dot(hc_ref[...], wout_ref[...], preferred_element_type=F32)
    o_ref[...] = x_ref[...] + y
    _cast_blocks(cast_src, cast_dst)


def _conv_mixer(x, gain, w_in, conv_k, w_out, layer, seq_len, cast_next):
    n, d = x.shape
    tm = CONV_ROW_TILE
    cc = MXU_DIM
    n_chunks = d // cc
    assert seq_len % tm == 0 and n % seq_len == 0 and CONV_WIDTH - 1 <= 8
    cast_in, cast_out, cast_shapes = _cast_plan(cast_next, n // tm)
    out, *casts = pl.pallas_call(
        functools.partial(_conv_kernel, n_chunks=n_chunks, cc=cc, tiles_per_seq=seq_len // tm,
                          n_cast=len(cast_next)),
        grid=(n // tm,),
        in_specs=[_row_spec(tm, d), _resident((1, d)), _resident((d, 3 * d)),
                  _resident_layer((CONV_WIDTH, d), layer), _resident((d, d))] + cast_in,
        out_specs=[_row_spec(tm, d)] + cast_out,
        out_shape=[jax.ShapeDtypeStruct((n, d), F32)] + cast_shapes,
        scratch_shapes=[pltpu.VMEM((tm, d), BF16), pltpu.VMEM((tm, d), BF16),
                        pltpu.VMEM((n_chunks, 8, cc), F32)],
        compiler_params=_params(),
        name="conv_mixer",
    )(x, gain.reshape(1, d), w_in, conv_k, w_out, *(w for w, _ in cast_next))
    return out, casts


def _rope_kernel(pos_ref, invf_ref, cos_ref, sin_ref):
    n_freq = HEAD_DIM // 2
    ang = pos_ref[...].astype(F32) * invf_ref[...]
    lane = lax.broadcasted_iota(jnp.int32, ang.shape, 1)
    sign = jnp.where((lane & n_freq) == 0, -1.0, 1.0)
    for dense, out_ref, factor in ((jnp.cos(ang), cos_ref, None), (jnp.sin(ang), sin_ref, sign)):
        for q in range(LANES // n_freq):
            t = dense if q == 0 else pltpu.roll(dense, LANES - q * n_freq, 1)
            t = jnp.where(lane < n_freq, t, pltpu.roll(t, n_freq, 1))
            t = jnp.where(lane < 2 * n_freq, t, pltpu.roll(t, 2 * n_freq, 1))
            out_ref[q] = t if factor is None else t * factor


def _rope_tables(positions_flat):
    n = positions_flat.shape[0]
    n_freq = HEAD_DIM // 2
    groups = LANES // n_freq
    rows = n // groups
    tm = min(ROW_TILE, rows)
    assert n % groups == 0 and rows % tm == 0
    inv_freq = ROPE_THETA ** (-jnp.arange(0, HEAD_DIM, 2, dtype=F32) / HEAD_DIM)
    invf = jnp.tile(inv_freq, groups).reshape(1, LANES)
    pos = jnp.repeat(positions_flat.reshape(groups, rows).T, n_freq, axis=1)
    out_spec = pl.BlockSpec((groups, tm, LANES), lambda i: (0, i, 0))
    cos_t, sin_t = pl.pallas_call(
        _rope_kernel,
        grid=(rows // tm,),
        in_specs=[_row_spec(tm, LANES), _resident((1, LANES))],
        out_specs=[out_spec, out_spec],
        out_shape=[jax.ShapeDtypeStruct((groups, rows, LANES), F32)] * 2,
        compiler_params=_params(),
        name="rope_tables",
    )(pos, invf)
    return cos_t.reshape(n, LANES), sin_t.reshape(n, LANES)


def _transpose_blocks(t):
    return jnp.concatenate([t[r:r + WINDOW, :].T for r in range(0, t.shape[0], WINDOW)], axis=1)


def _attn_kernel(*refs, tiles_per_seq, n_parts, n_cast):
    sinks_ref, x_ref, g_ref, cos_ref, sin_ref, wqkv_ref, wo_ref = refs[:7]
    cast_src = refs[7:7 + n_cast]
    o_ref = refs[7 + n_cast]
    cast_dst = refs[8 + n_cast:8 + 2 * n_cast]
    xn_ref, q_ref, ka_ref, kb_ref, vat_ref, vbt_ref, att_ref = refs[8 + 2 * n_cast:]
    tm, d = x_ref.shape
    part = tm // n_parts
    blocks_per_part = part // WINDOW
    n_q_slabs = d // LANES
    n_kv_slabs = N_KV_HEADS * HEAD_DIM // LANES
    i = pl.program_id(0)
    first = i % tiles_per_seq == 0

    @pl.when(first)
    def _():
        for r in (ka_ref, kb_ref):
            r[:, 0:WINDOW, :] = jnp.zeros((N_KV_HEADS, WINDOW, LANES), BF16)
        for r in (vat_ref, vbt_ref):
            r[:, 0:LANES, 0:WINDOW] = jnp.zeros((N_KV_HEADS, LANES, WINDOW), BF16)

    @pl.when(jnp.logical_not(first))
    def _():
        for r in (ka_ref, kb_ref):
            r[:, 0:WINDOW, :] = r[:, tm:tm + WINDOW, :]
        for r in (vat_ref, vbt_ref):
            r[:, 0:LANES, 0:WINDOW] = r[:, 0:LANES, tm:tm + WINDOW]

    extra = lax.broadcasted_iota(jnp.int32, (BF16_SUBLANES, tm + WINDOW), 0)
    for r, row in ((vat_ref, 0), (vbt_ref, 1)):
        ones_row = jnp.where(extra == row, 1.0, 0.0).astype(BF16)
        for head in range(N_KV_HEADS):
            r[head, LANES:, :] = ones_row

    lane = lax.broadcasted_iota(jnp.int32, (part, LANES), 1)
    first_half = (lane & (HEAD_DIM // 2)) == 0
    low_head = lane < HEAD_DIM
    zero = jnp.zeros((part, LANES), F32)
    zero_t = jnp.zeros((HEAD_DIM, part), BF16)
    q_scale = LOG2_E / math.sqrt(HEAD_DIM)

    def rope(t, rows):
        partner = jnp.where(first_half,
                            pltpu.roll(t, LANES - HEAD_DIM // 2, 1),
                            pltpu.roll(t, HEAD_DIM // 2, 1))
        return t * cos_ref[rows, :] + partner * sin_ref[rows, :]

    def project_chunk(p, c):
        rows = slice(p * part, (p + 1) * part)
        halo_rows = slice(WINDOW + p * part, WINDOW + (p + 1) * part)
        if c == 0:
            xn_ref[rows, :] = _rmsnorm(x_ref[rows, :], g_ref[...]).astype(BF16)
        t2 = jnp.dot(xn_ref[rows, :], wqkv_ref[:, c * MXU_DIM:(c + 1) * MXU_DIM],
                     preferred_element_type=F32)
        for half in range(MXU_DIM // LANES):
            s = c * (MXU_DIM // LANES) + half
            t = t2[:, half * LANES:(half + 1) * LANES]
            if s < n_q_slabs:
                q_ref[rows, s * LANES:(s + 1) * LANES] = (rope(t, rows) * q_scale).astype(BF16)
            elif s < n_q_slabs + n_kv_slabs:
                ks = s - n_q_slabs
                k_slab = rope(t, rows)
                swapped = pltpu.roll(k_slab, HEAD_DIM, 1)
                ka_ref[2 * ks, halo_rows, :] = jnp.where(low_head, k_slab, zero).astype(BF16)
                kb_ref[2 * ks, halo_rows, :] = jnp.where(low_head, zero, swapped).astype(BF16)
                ka_ref[2 * ks + 1, halo_rows, :] = jnp.where(low_head, swapped, zero).astype(BF16)
                kb_ref[2 * ks + 1, halo_rows, :] = jnp.where(low_head, zero, k_slab).astype(BF16)
            else:
                vs = s - n_q_slabs - n_kv_slabs
                v_t = _transpose_blocks(t).astype(BF16)
                for head, v_head in ((2 * vs, v_t[:HEAD_DIM, :]), (2 * vs + 1, v_t[HEAD_DIM:, :])):
                    vat_ref[head, 0:LANES, halo_rows] = jnp.concatenate([v_head, zero_t], axis=0)
                    vbt_ref[head, 0:LANES, halo_rows] = jnp.concatenate([zero_t, v_head], axis=0)

    def out_chunk(p, c):
        rows = slice(p * part, (p + 1) * part)
        cols = slice(c * MXU_DIM, (c + 1) * MXU_DIM)
        y = jnp.dot(att_ref[rows, :], wo_ref[:, cols], preferred_element_type=F32)
        o_ref[rows, cols] = x_ref[rows, cols] + y

    project_chunks = wqkv_ref.shape[1] // MXU_DIM
    out_chunks = d // MXU_DIM

    kj = lax.broadcasted_iota(jnp.int32, (WINDOW, WINDOW), 0)
    qi = lax.broadcasted_iota(jnp.int32, (WINDOW, WINDOW), 1)
    cur_ok = kj <= qi
    prev_ok = kj > qi
    prev_ok_first = kj > qi + jnp.where(first, WINDOW, 0)
    neg = jnp.finfo(F32).min

    def scores(b, g):
        rows = slice(b * WINDOW, (b + 1) * WINDOW)
        keys = slice(b * WINDOW, b * WINDOW + 2 * WINDOW)
        q2 = jnp.concatenate([q_ref[rows, sl * LANES:(sl + 1) * LANES]
                              for sl in (2 * g, 2 * g + 1)], axis=0)
        k2 = jnp.concatenate([ka_ref[g, keys, :], kb_ref[g, keys, :]], axis=0)
        return lax.dot_general(k2, q2, (((1,), (1,)), ((), ())),
                               preferred_element_type=F32)

    def attend(b, g, s_t):
        rows = slice(b * WINDOW, (b + 1) * WINDOW)
        keys = slice(b * WINDOW, b * WINDOW + 2 * WINDOW)
        mask = jnp.concatenate([prev_ok_first if b == 0 else prev_ok, cur_ok], axis=0)
        p_cols = []
        sink_terms = []
        for pair in range(2):
            p_col = []
            sink_col = []
            for lo in range(2):
                sink = sinks_ref[GROUP * g + 2 * pair + lo] * LOG2_E
                sc = s_t[lo * 2 * WINDOW:(lo + 1) * 2 * WINDOW, pair * WINDOW:(pair + 1) * WINDOW]
                sc = jnp.where(mask, sc, neg)
                m = jnp.maximum(jnp.max(sc, axis=0, keepdims=True), sink)
                p_col.append(jnp.exp2(sc - m).astype(BF16))
                sink_col.append(jnp.exp2(sink - m))
            p_cols.append(jnp.concatenate(p_col, axis=0))
            sink_terms.append(sink_col)
        p_t = jnp.concatenate(p_cols, axis=1)
        v2_t = jnp.concatenate([vat_ref[g, :, keys], vbt_ref[g, :, keys]], axis=1)
        o_t = jnp.dot(v2_t, p_t, preferred_element_type=F32)
        for pair in range(2):
            cols = slice(pair * WINDOW, (pair + 1) * WINDOW)
            slab = 2 * g + pair
            inv = [1.0 / (o_t[LANES + lo:LANES + lo + 1, cols] + sink_terms[pair][lo])
                   for lo in range(2)]
            o_pair = jnp.concatenate([o_t[:HEAD_DIM, cols] * inv[0],
                                      o_t[HEAD_DIM:LANES, cols] * inv[1]], axis=0)
            att_ref[rows, slab * LANES:(slab + 1) * LANES] = o_pair.T.astype(BF16)

    units = [(b, g) for b in range(tm // WINDOW) for g in range(N_KV_HEADS)]
    units_per_part = blocks_per_part * N_KV_HEADS
    filler_slots = units_per_part - SCORE_LOOKAHEAD

    for c in range(project_chunks):
        project_chunk(0, c)
    pending = [scores(*u) for u in units[:SCORE_LOOKAHEAD]]
    fillers = []
    for n, (b, g) in enumerate(units):
        p, j = divmod(n, units_per_part)
        if j == 0:
            fillers = []
            if p + 1 < n_parts:
                fillers += [functools.partial(project_chunk, p + 1, c)
                            for c in range(project_chunks)]
            if p >= 1:
                fillers += [functools.partial(out_chunk, p - 1, c) for c in range(out_chunks)]
        if n + SCORE_LOOKAHEAD < len(units):
            pending.append(scores(*units[n + SCORE_LOOKAHEAD]))
        attend(b, g, pending[n])
        pending[n] = None
        if j < filler_slots:
            lo_f = len(fillers) * j // filler_slots
            hi_f = len(fillers) * (j + 1) // filler_slots
            for filler in fillers[lo_f:hi_f]:
                filler()
    for c in range(out_chunks):
        out_chunk(n_parts - 1, c)
    _cast_blocks(cast_src, cast_dst)


def _attention(x, gain, cos_t, sin_t, w_qkv, sinks, w_o, seq_len, cast_next):
    n, d = x.shape
    tm = ATTN_ROW_TILE
    n_parts = ATTN_PARTS
    qkv_w = w_qkv.shape[1]
    assert seq_len % tm == 0 and tm % (n_parts * WINDOW) == 0 and d == N_Q_HEADS * HEAD_DIM
    assert qkv_w % MXU_DIM == 0 and (tm // n_parts // WINDOW) * N_KV_HEADS > SCORE_LOOKAHEAD
    k_halo = pltpu.VMEM((N_KV_HEADS, tm + WINDOW, LANES), BF16)
    vt_halo = pltpu.VMEM((N_KV_HEADS, LANES + BF16_SUBLANES, tm + WINDOW), BF16)
    cast_in, cast_out, cast_shapes = _cast_plan(cast_next, n // tm)
    out, *casts = pl.pallas_call(
        functools.partial(_attn_kernel, tiles_per_seq=seq_len // tm, n_parts=n_parts,
                          n_cast=len(cast_next)),
        grid=(n // tm,),
        in_specs=[pl.BlockSpec(memory_space=pltpu.SMEM),
                  _row_spec(tm, d), _resident((1, d)), _row_spec(tm, LANES), _row_spec(tm, LANES),
                  _resident((d, qkv_w)), _resident((d, d))] + cast_in,
        out_specs=[_row_spec(tm, d)] + cast_out,
        out_shape=[jax.ShapeDtypeStruct((n, d), F32)] + cast_shapes,
        scratch_shapes=[pltpu.VMEM((tm, d), BF16), pltpu.VMEM((tm, d), BF16),
                        k_halo, k_halo, vt_halo, vt_halo,
                        pltpu.VMEM((tm, d), BF16)],
        compiler_params=_params(),
        name="swa_attention",
    )(sinks, x, gain.reshape(1, d), cos_t, sin_t, w_qkv, w_o, *(w for w, _ in cast_next))
    return out, casts


def kernel(x, positions, ln_ff1, w_ff1_in, w_ff1_out, ln_mix, ln_ff2, w_ff2_in, w_ff2_out,
           conv_w_in, conv_kernel, conv_w_out, attn_w_qkv, attn_sinks, attn_w_o, ln_final):
    batch, seq_len, d = x.shape
    depth = ln_ff1.shape[0]
    h = x.reshape(batch * seq_len, d)
    cos_t, sin_t = _rope_tables(positions.reshape(batch * seq_len))

    stages = []
    for i in range(depth):
        j = i // 2
        stages.append(("ffn", [(w_ff1_in, i), (w_ff1_out, i)],
                       dict(gain=ln_ff1[i], final_gain=None)))
        if i % 2 == 0:
            stages.append(("conv", [(conv_w_in, j), (conv_w_out, j)],
                           dict(gain=ln_mix[i], layer=j)))
        else:
            stages.append(("attn", [(attn_w_qkv, j), (attn_w_o, j)],
                           dict(gain=ln_mix[i], sinks=attn_sinks[j])))
        stages.append(("ffn", [(w_ff2_in, i), (w_ff2_out, i)],
                       dict(gain=ln_ff2[i], final_gain=ln_final if i == depth - 1 else None)))

    weights = [w[layer].astype(BF16) for w, layer in stages[0][1]]
    for k, (kind, _, args) in enumerate(stages):
        cast_next = stages[k + 1][1] if k + 1 < len(stages) else []
        if kind == "ffn":
            h, weights = _ffn(h, args["gain"], *weights, cast_next,
                              final_gain=args["final_gain"])
        elif kind == "conv":
            h, weights = _conv_mixer(h, args["gain"], weights[0], conv_kernel, weights[1],
                                     args["layer"], seq_len, cast_next)
        else:
            h, weights = _attention(h, args["gain"], cos_t, sin_t, weights[0], args["sinks"],
                                    weights[1], seq_len, cast_next)
    return h.reshape(batch, seq_len, d)
```

```python
import functools
import math

import jax
import jax.numpy as jnp
from jax import lax
from jax.experimental import pallas as pl
from jax.experimental.pallas import tpu as pltpu

RMS_EPS = 1e-6
ROPE_THETA = 10000.0
HEAD_DIM = 64
N_Q_HEADS = 16
N_KV_HEADS = 4
GROUP = N_Q_HEADS // N_KV_HEADS
WINDOW = 128
CONV_WIDTH = 3
LANES = 128
BF16_SUBLANES = 16
MXU_DIM = 256

ROW_TILE = 512
CONV_ROW_TILE = 1024
FFN_ROW_TILE = 1024
FFN_CHUNK = 256
NORM_PARTS = 4
ATTN_ROW_TILE = 1024
ATTN_PARTS = 4
SCORE_LOOKAHEAD = 2
LOG2_E = math.log2(math.e)
VMEM_LIMIT_BYTES = 56 * 1024 * 1024

F32 = jnp.float32
BF16 = jnp.bfloat16


def _rmsnorm(x, g):
    ms = jnp.mean(x * x, axis=-1, keepdims=True)
    return (x * lax.rsqrt(ms + RMS_EPS)) * g


def _resident(shape):
    return pl.BlockSpec(shape, lambda i: (0,) * len(shape), pipeline_mode=pl.Buffered(1))


def _resident_layer(shape, layer):
    return pl.BlockSpec((None,) + shape, lambda i: (layer,) + (0,) * len(shape),
                        pipeline_mode=pl.Buffered(1))


def _row_spec(tm, width):
    return pl.BlockSpec((tm, width), lambda i: (i, 0))


def _params():
    return pltpu.CompilerParams(dimension_semantics=("arbitrary",),
                                vmem_limit_bytes=VMEM_LIMIT_BYTES)


def _cast_plan(weights, n_steps):
    in_specs, out_specs, out_shapes = [], [], []
    for w, layer in weights:
        _, rows, cols = w.shape
        block_rows = next(r for r in range(BF16_SUBLANES, rows + 1, BF16_SUBLANES)
                          if rows % r == 0 and rows // r <= n_steps)
        last = rows // block_rows - 1
        in_specs.append(pl.BlockSpec((None, block_rows, cols),
                                     lambda i, layer=layer, last=last: (layer, jnp.minimum(i, last), 0)))
        out_specs.append(pl.BlockSpec((block_rows, cols),
                                      lambda i, last=last: (jnp.minimum(i, last), 0)))
        out_shapes.append(jax.ShapeDtypeStruct((rows, cols), BF16))
    return in_specs, out_specs, out_shapes


def _cast_blocks(src_refs, dst_refs):
    for src, dst in zip(src_refs, dst_refs):
        dst[...] = src[...].astype(BF16)


def _ffn_kernel(*refs, fc, final_norm, n_cast):
    n_in = 5 if final_norm else 4
    x_ref, g_ref, win_ref, wo_ref = refs[:4]
    gf_ref = refs[4] if final_norm else None
    cast_src = refs[n_in:n_in + n_cast]
    o_ref = refs[n_in + n_cast]
    cast_dst = refs[n_in + n_cast + 1:n_in + 2 * n_cast + 1]
    xn_ref, h_ref = refs[n_in + 2 * n_cast + 1:]
    tm = x_ref.shape[0]
    f = wo_ref.shape[0]

    def hidden(rows, lo, hi):
        gate = jnp.dot(xn_ref[rows, :], win_ref[:, lo:hi], preferred_element_type=F32)
        up = jnp.dot(xn_ref[rows, :], win_ref[:, f + lo:f + hi], preferred_element_type=F32)
        h_ref[rows, lo:hi] = ((gate / (1.0 + jnp.exp(-gate))) * up).astype(BF16)

    part = tm // NORM_PARTS
    for r in range(0, tm, part):
        rows = slice(r, r + part)
        xn_ref[rows, :] = _rmsnorm(x_ref[rows, :], g_ref[...]).astype(BF16)
        hidden(rows, 0, fc)
    for lo in range(fc, f, fc):
        hidden(slice(None), lo, min(lo + fc, f))
    y = jnp.dot(h_ref[...], wo_ref[...], preferred_element_type=F32)
    out = x_ref[...] + 0.5 * y
    if final_norm:
        out = _rmsnorm(out, gf_ref[...])
    o_ref[...] = out
    _cast_blocks(cast_src, cast_dst)


def _ffn(x, gain, w_in, w_out, cast_next, final_gain=None):
    n, d = x.shape
    f = w_out.shape[0]
    fc = FFN_CHUNK
    tm = FFN_ROW_TILE
    assert f % MXU_DIM == 0 and fc % MXU_DIM == 0 and n % tm == 0
    final_norm = final_gain is not None
    in_specs = [_row_spec(tm, d), _resident((1, d)), _resident((d, 2 * f)), _resident((f, d))]
    args = [x, gain.reshape(1, d), w_in, w_out]
    if final_norm:
        in_specs.append(_resident((1, d)))
        args.append(final_gain.reshape(1, d))
    cast_in, cast_out, cast_shapes = _cast_plan(cast_next, n // tm)
    out, *casts = pl.pallas_call(
        functools.partial(_ffn_kernel, fc=fc, final_norm=final_norm, n_cast=len(cast_next)),
        grid=(n // tm,),
        in_specs=in_specs + cast_in,
        out_specs=[_row_spec(tm, d)] + cast_out,
        out_shape=[jax.ShapeDtypeStruct((n, d), F32)] + cast_shapes,
        scratch_shapes=[pltpu.VMEM((tm, d), BF16), pltpu.VMEM((tm, f), BF16)],
        compiler_params=_params(),
        name="ffn_final" if final_norm else "ffn",
    )(*args, *(w for w, _ in cast_next))
    return out, casts


def _conv_kernel(*refs, n_chunks, cc, tiles_per_seq, n_cast):
    x_ref, g_ref, win_ref, ck_ref, wout_ref = refs[:5]
    cast_src = refs[5:5 + n_cast]
    o_ref = refs[5 + n_cast]
    cast_dst = refs[6 + n_cast:6 + 2 * n_cast]
    xn_ref, hc_ref, carry_ref = refs[6 + 2 * n_cast:]
    tm = x_ref.shape[0]
    i = pl.program_id(0)

    @pl.when(i % tiles_per_seq == 0)
    def _():
        carry_ref[...] = jnp.zeros_like(carry_ref)

    d = n_chunks * cc
    xn_ref[...] = _rmsnorm(x_ref[...], g_ref[...]).astype(BF16)
    row = lax.broadcasted_iota(jnp.int32, (tm, cc), 0)
    for j in range(n_chunks):
        cols = slice(j * cc, (j + 1) * cc)
        gate_b, gate_c, u = (
            jnp.dot(xn_ref[...], win_ref[:, part * d + j * cc:part * d + (j + 1) * cc],
                    preferred_element_type=F32) for part in range(3))
        z = gate_c * u
        prev = carry_ref[j]
        carry_ref[j] = z[tm - 8:, :]
        p1 = jnp.broadcast_to(prev[7:8, :], (tm, cc))
        p2 = jnp.broadcast_to(prev[6:7, :], (tm, cc))
        z1 = jnp.where(row == 0, p1, pltpu.roll(z, 1, 0))
        z2 = jnp.where(row == 0, p2, jnp.where(row == 1, p1, pltpu.roll(z, 2, 0)))
        k = ck_ref[:, cols]
        conv = k[0:1, :] * z2
        conv = conv + k[1:2, :] * z1
        conv = conv + k[2:3, :] * z
        hc_ref[:, cols] = (gate_b * conv).astype(BF16)
    y = jnp.dot(hc_ref[...], wout_ref[...], preferred_element_type=F32)
    o_ref[...] = x_ref[...] + y
    _cast_blocks(cast_src, cast_dst)


def _conv_mixer(x, gain, w_in, conv_k, w_out, layer, seq_len, cast_next):
    n, d = x.shape
    tm = CONV_ROW_TILE
    cc = MXU_DIM
    n_chunks = d // cc
    assert seq_len % tm == 0 and n % seq_len == 0 and CONV_WIDTH - 1 <= 8
    cast_in, cast_out, cast_shapes = _cast_plan(cast_next, n // tm)
    out, *casts = pl.pallas_call(
        functools.partial(_conv_kernel, n_chunks=n_chunks, cc=cc, tiles_per_seq=seq_len // tm,
                          n_cast=len(cast_next)),
        grid=(n // tm,),
        in_specs=[_row_spec(tm, d), _resident((1, d)), _resident((d, 3 * d)),
                  _resident_layer((CONV_WIDTH, d), layer), _resident((d, d))] + cast_in,
        out_specs=[_row_spec(tm, d)] + cast_out,
        out_shape=[jax.ShapeDtypeStruct((n, d), F32)] + cast_shapes,
        scratch_shapes=[pltpu.VMEM((tm, d), BF16), pltpu.VMEM((tm, d), BF16),
                        pltpu.VMEM((n_chunks, 8, cc), F32)],
        compiler_params=_params(),
        name="conv_mixer",
    )(x, gain.reshape(1, d), w_in, conv_k, w_out, *(w for w, _ in cast_next))
    return out, casts


def _rope_kernel(*refs, n_cast):
    pos_ref, invf_ref = refs[:2]
    cast_src = refs[2:2 + n_cast]
    cos_ref, sin_ref = refs[2 + n_cast:4 + n_cast]
    cast_dst = refs[4 + n_cast:]
    _cast_blocks(cast_src, cast_dst)
    n_freq = HEAD_DIM // 2
    ang = pos_ref[...].astype(F32) * invf_ref[...]
    lane = lax.broadcasted_iota(jnp.int32, ang.shape, 1)
    sign = jnp.where((lane & n_freq) == 0, -1.0, 1.0)
    groups = LANES // n_freq
    group = lax.shift_right_logical(lane, n_freq.bit_length() - 1)
    for dense, out_ref, factor in ((jnp.cos(ang), cos_ref, None), (jnp.sin(ang), sin_ref, sign)):
        rolled = [dense] + [pltpu.roll(dense, s * n_freq, 1) for s in range(1, groups)]
        for q in range(groups):
            t = rolled[(groups - 1 - q) % groups]
            for g in range(groups - 1):
                t = jnp.where(group == g, rolled[(g - q) % groups], t)
            out_ref[q] = t if factor is None else t * factor


def _rope_tables(positions_flat, cast_next):
    n = positions_flat.shape[0]
    n_freq = HEAD_DIM // 2
    groups = LANES // n_freq
    rows = n // groups
    tm = min(ROW_TILE, rows)
    assert n % groups == 0 and rows % tm == 0
    inv_freq = ROPE_THETA ** (-jnp.arange(0, HEAD_DIM, 2, dtype=F32) / HEAD_DIM)
    invf = jnp.tile(inv_freq, groups).reshape(1, LANES)
    pos = jnp.repeat(positions_flat.reshape(groups, rows).T, n_freq, axis=1)
    out_spec = pl.BlockSpec((groups, tm, LANES), lambda i: (0, i, 0))
    cast_in, cast_out, cast_shapes = _cast_plan(cast_next, rows // tm)
    cos_t, sin_t, *casts = pl.pallas_call(
        functools.partial(_rope_kernel, n_cast=len(cast_next)),
        grid=(rows // tm,),
        in_specs=[_row_spec(tm, LANES), _resident((1, LANES))] + cast_in,
        out_specs=[out_spec, out_spec] + cast_out,
        out_shape=[jax.ShapeDtypeStruct((groups, rows, LANES), F32)] * 2 + cast_shapes,
        compiler_params=_params(),
        name="rope_tables",
    )(pos, invf, *(w for w, _ in cast_next))
    return cos_t.reshape(n, LANES), sin_t.reshape(n, LANES), casts


def _transpose_blocks(t):
    return jnp.concatenate([t[r:r + WINDOW, :].T for r in range(0, t.shape[0], WINDOW)], axis=1)


def _attn_kernel(*refs, tiles_per_seq, n_parts, n_cast):
    sinks_ref, x_ref, g_ref, cos_ref, sin_ref, wqkv_ref, wo_ref = refs[:7]
    cast_src = refs[7:7 + n_cast]
    o_ref = refs[7 + n_cast]
    cast_dst = refs[8 + n_cast:8 + 2 * n_cast]
    xn_ref, q_ref, ka_ref, kb_ref, vat_ref, vbt_ref, att_ref = refs[8 + 2 * n_cast:]
    tm, d = x_ref.shape
    part = tm // n_parts
    blocks_per_part = part // WINDOW
    n_q_slabs = d // LANES
    n_kv_slabs = N_KV_HEADS * HEAD_DIM // LANES
    i = pl.program_id(0)
    first = i % tiles_per_seq == 0

    @pl.when(first)
    def _():
        for r in (ka_ref, kb_ref):
            r[:, 0:WINDOW, :] = jnp.zeros((N_KV_HEADS, WINDOW, LANES), BF16)
        for r in (vat_ref, vbt_ref):
            r[:, 0:LANES, 0:WINDOW] = jnp.zeros((N_KV_HEADS, LANES, WINDOW), BF16)

    @pl.when(jnp.logical_not(first))
    def _():
        for r in (ka_ref, kb_ref):
            r[:, 0:WINDOW, :] = r[:, tm:tm + WINDOW, :]
        for r in (vat_ref, vbt_ref):
            r[:, 0:LANES, 0:WINDOW] = r[:, 0:LANES, tm:tm + WINDOW]

    extra = lax.broadcasted_iota(jnp.int32, (BF16_SUBLANES, tm + WINDOW), 0)
    for r, row in ((vat_ref, 0), (vbt_ref, 1)):
        ones_row = jnp.where(extra == row, 1.0, 0.0).astype(BF16)
        for head in range(N_KV_HEADS):
            r[head, LANES:, :] = ones_row

    lane = lax.broadcasted_iota(jnp.int32, (part, LANES), 1)
    first_half = (lane & (HEAD_DIM // 2)) == 0
    low_head = lane < HEAD_DIM
    zero = jnp.zeros((part, LANES), F32)
    zero_t = jnp.zeros((HEAD_DIM, part), BF16)
    q_scale = LOG2_E / math.sqrt(HEAD_DIM)

    def rope(t, rows):
        partner = jnp.where(first_half,
                            pltpu.roll(t, LANES - HEAD_DIM // 2, 1),
                            pltpu.roll(t, HEAD_DIM // 2, 1))
        return t * cos_ref[rows, :] + partner * sin_ref[rows, :]

    def project_chunk(p, c):
        rows = slice(p * part, (p + 1) * part)
        halo_rows = slice(WINDOW + p * part, WINDOW + (p + 1) * part)
        if c == 0:
            xn_ref[rows, :] = _rmsnorm(x_ref[rows, :], g_ref[...]).astype(BF16)
        t2 = jnp.dot(xn_ref[rows, :], wqkv_ref[:, c * MXU_DIM:(c + 1) * MXU_DIM],
                     preferred_element_type=F32)
        for half in range(MXU_DIM // LANES):
            s = c * (MXU_DIM // LANES) + half
            t = t2[:, half * LANES:(half + 1) * LANES]
            if s < n_q_slabs:
                q_ref[rows, s * LANES:(s + 1) * LANES] = (rope(t, rows) * q_scale).astype(BF16)
            elif s < n_q_slabs + n_kv_slabs:
                ks = s - n_q_slabs
                k_slab = rope(t, rows)
                swapped = pltpu.roll(k_slab, HEAD_DIM, 1)
                ka_ref[2 * ks, halo_rows, :] = jnp.where(low_head, k_slab, zero).astype(BF16)
                kb_ref[2 * ks, halo_rows, :] = jnp.where(low_head, zero, swapped).astype(BF16)
                ka_ref[2 * ks + 1, halo_rows, :] = jnp.where(low_head, swapped, zero).astype(BF16)
                kb_ref[2 * ks + 1, halo_rows, :] = jnp.where(low_head, zero, k_slab).astype(BF16)
            else:
                vs = s - n_q_slabs - n_kv_slabs
                v_t = _transpose_blocks(t).astype(BF16)
                for head, v_head in ((2 * vs, v_t[:HEAD_DIM, :]), (2 * vs + 1, v_t[HEAD_DIM:, :])):
                    vat_ref[head, 0:LANES, halo_rows] = jnp.concatenate([v_head, zero_t], axis=0)
                    vbt_ref[head, 0:LANES, halo_rows] = jnp.concatenate([zero_t, v_head], axis=0)

    def out_chunk(p, c):
        rows = slice(p * part, (p + 1) * part)
        cols = slice(c * MXU_DIM, (c + 1) * MXU_DIM)
        y = jnp.dot(att_ref[rows, :], wo_ref[:, cols], preferred_element_type=F32)
        o_ref[rows, cols] = x_ref[rows, cols] + y

    project_chunks = wqkv_ref.shape[1] // MXU_DIM
    out_chunks = d // MXU_DIM

    kj = lax.broadcasted_iota(jnp.int32, (WINDOW, WINDOW), 0)
    qi = lax.broadcasted_iota(jnp.int32, (WINDOW, WINDOW), 1)
    cur_ok = kj <= qi
    prev_ok = kj > qi
    prev_ok_first = kj > qi + jnp.where(first, WINDOW, 0)
    neg = jnp.finfo(F32).min

    def scores(b, g):
        rows = slice(b * WINDOW, (b + 1) * WINDOW)
        keys = slice(b * WINDOW, b * WINDOW + 2 * WINDOW)
        q2 = jnp.concatenate([q_ref[rows, sl * LANES:(sl + 1) * LANES]
                              for sl in (2 * g, 2 * g + 1)], axis=0)
        k2 = jnp.concatenate([ka_ref[g, keys, :], kb_ref[g, keys, :]], axis=0)
        return lax.dot_general(k2, q2, (((1,), (1,)), ((), ())),
                               preferred_element_type=F32)

    def attend(b, g, s_t):
        rows = slice(b * WINDOW, (b + 1) * WINDOW)
        keys = slice(b * WINDOW, b * WINDOW + 2 * WINDOW)
        mask = jnp.concatenate([prev_ok_first if b == 0 else prev_ok, cur_ok], axis=0)
        p_cols = []
        sink_terms = []
        for pair in range(2):
            p_col = []
            sink_col = []
            for lo in range(2):
                sink = sinks_ref[GROUP * g + 2 * pair + lo] * LOG2_E
                sc = s_t[lo * 2 * WINDOW:(lo + 1) * 2 * WINDOW, pair * WINDOW:(pair + 1) * WINDOW]
                sc = jnp.where(mask, sc, neg)
                m = jnp.maximum(jnp.max(sc, axis=0, keepdims=True), sink)
                p_col.append(jnp.exp2(sc - m).astype(BF16))
                sink_col.append(jnp.exp2(sink - m))
            p_cols.append(jnp.concatenate(p_col, axis=0))
            sink_terms.append(sink_col)
        p_t = jnp.concatenate(p_cols, axis=1)
        v2_t = jnp.concatenate([vat_ref[g, :, keys], vbt_ref[g, :, keys]], axis=1)
        o_t = jnp.dot(v2_t, p_t, preferred_element_type=F32)
        for pair in range(2):
            cols = slice(pair * WINDOW, (pair + 1) * WINDOW)
            slab = 2 * g + pair
            inv = [1.0 / (o_t[LANES + lo:LANES + lo + 1, cols] + sink_terms[pair][lo])
                   for lo in range(2)]
            o_pair = jnp.concatenate([o_t[:HEAD_DIM, cols] * inv[0],
                                      o_t[HEAD_DIM:LANES, cols] * inv[1]], axis=0)
            att_ref[rows, slab * LANES:(slab + 1) * LANES] = o_pair.T.astype(BF16)

    units = [(b, g) for b in range(tm // WINDOW) for g in range(N_KV_HEADS)]
    units_per_part = blocks_per_part * N_KV_HEADS
    filler_slots = units_per_part - SCORE_LOOKAHEAD

    for c in range(project_chunks):
        project_chunk(0, c)
    pending = [scores(*u) for u in units[:SCORE_LOOKAHEAD]]
    fillers = []
    for n, (b, g) in enumerate(units):
        p, j = divmod(n, units_per_part)
        if j == 0:
            fillers = []
            if p + 1 < n_parts:
                fillers += [functools.partial(project_chunk, p + 1, c)
                            for c in range(project_chunks)]
            if p >= 1:
                fillers += [functools.partial(out_chunk, p - 1, c) for c in range(out_chunks)]
        if n + SCORE_LOOKAHEAD < len(units):
            pending.append(scores(*units[n + SCORE_LOOKAHEAD]))
        attend(b, g, pending[n])
        pending[n] = None
        if j < filler_slots:
            lo_f = len(fillers) * j // filler_slots
            hi_f = len(fillers) * (j + 1) // filler_slots
            for filler in fillers[lo_f:hi_f]:
                filler()
    for c in range(out_chunks):
        out_chunk(n_parts - 1, c)
    _cast_blocks(cast_src, cast_dst)


def _attention(x, gain, cos_t, sin_t, w_qkv, sinks, w_o, seq_len, cast_next):
    n, d = x.shape
    tm = ATTN_ROW_TILE
    n_parts = ATTN_PARTS
    qkv_w = w_qkv.shape[1]
    assert seq_len % tm == 0 and tm % (n_parts * WINDOW) == 0 and d == N_Q_HEADS * HEAD_DIM
    assert qkv_w % MXU_DIM == 0 and (tm // n_parts // WINDOW) * N_KV_HEADS > SCORE_LOOKAHEAD
    k_halo = pltpu.VMEM((N_KV_HEADS, tm + WINDOW, LANES), BF16)
    vt_halo = pltpu.VMEM((N_KV_HEADS, LANES + BF16_SUBLANES, tm + WINDOW), BF16)
    cast_in, cast_out, cast_shapes = _cast_plan(cast_next, n // tm)
    out, *casts = pl.pallas_call(
        functools.partial(_attn_kernel, tiles_per_seq=seq_len // tm, n_parts=n_parts,
                          n_cast=len(cast_next)),
        grid=(n // tm,),
        in_specs=[pl.BlockSpec(memory_space=pltpu.SMEM),
                  _row_spec(tm, d), _resident((1, d)), _row_spec(tm, LANES), _row_spec(tm, LANES),
                  _resident((d, qkv_w)), _resident((d, d))] + cast_in,
        out_specs=[_row_spec(tm, d)] + cast_out,
        out_shape=[jax.ShapeDtypeStruct((n, d), F32)] + cast_shapes,
        scratch_shapes=[pltpu.VMEM((tm, d), BF16), pltpu.VMEM((tm, d), BF16),
                        k_halo, k_halo, vt_halo, vt_halo,
                        pltpu.VMEM((tm, d), BF16)],
        compiler_params=_params(),
        name="swa_attention",
    )(sinks, x, gain.reshape(1, d), cos_t, sin_t, w_qkv, w_o, *(w for w, _ in cast_next))
    return out, casts


def kernel(x, positions, ln_ff1, w_ff1_in, w_ff1_out, ln_mix, ln_ff2, w_ff2_in, w_ff2_out,
           conv_w_in, conv_kernel, conv_w_out, attn_w_qkv, attn_sinks, attn_w_o, ln_final):
    batch, seq_len, d = x.shape
    depth = ln_ff1.shape[0]
    h = x.reshape(batch * seq_len, d)

    stages = []
    for i in range(depth):
        j = i // 2
        stages.append(("ffn", [(w_ff1_in, i), (w_ff1_out, i)],
                       dict(gain=ln_ff1[i], final_gain=None)))
        if i % 2 == 0:
            stages.append(("conv", [(conv_w_in, j), (conv_w_out, j)],
                           dict(gain=ln_mix[i], layer=j)))
        else:
            stages.append(("attn", [(attn_w_qkv, j), (attn_w_o, j)],
                           dict(gain=ln_mix[i], sinks=attn_sinks[j])))
        stages.append(("ffn", [(w_ff2_in, i), (w_ff2_out, i)],
                       dict(gain=ln_ff2[i], final_gain=ln_final if i == depth - 1 else None)))

    cos_t, sin_t, weights = _rope_tables(positions.reshape(batch * seq_len), stages[0][1])
    for k, (kind, _, args) in enumerate(stages):
        cast_next = stages[k + 1][1] if k + 1 < len(stages) else []
        if kind == "ffn":
            h, weights = _ffn(h, args["gain"], *weights, cast_next,
                              final_gain=args["final_gain"])
        elif kind == "conv":
            h, weights = _conv_mixer(h, args["gain"], weights[0], conv_kernel, weights[1],
                                     args["layer"], seq_len, cast_next)
        else:
            h, weights = _attention(h, args["gain"], cos_t, sin_t, weights[0], args["sinks"],
                                    weights[1], seq_len, cast_next)
    return h.reshape(batch, seq_len, d)
```

```python
import functools
import math

import jax
import jax.numpy as jnp
from jax import lax
from jax.experimental import pallas as pl
from jax.experimental.pallas import tpu as pltpu

RMS_EPS = 1e-6
ROPE_THETA = 10000.0
HEAD_DIM = 64
N_Q_HEADS = 16
N_KV_HEADS = 4
GROUP = N_Q_HEADS // N_KV_HEADS
WINDOW = 128
CONV_WIDTH = 3
LANES = 128
BF16_SUBLANES = 16
MXU_DIM = 256

ROW_TILE = 512
CONV_ROW_TILE = 1024
FFN_ROW_TILE = 1024
FFN_CHUNK = 256
NORM_PARTS = 4
ATTN_ROW_TILE = 1024
ATTN_PARTS = 4
SCORE_LOOKAHEAD = 2
LOG2_E = math.log2(math.e)
VMEM_LIMIT_BYTES = 56 * 1024 * 1024

F32 = jnp.float32
BF16 = jnp.bfloat16


def _rmsnorm(x, g):
    ms = jnp.mean(x * x, axis=-1, keepdims=True)
    return (x * lax.rsqrt(ms + RMS_EPS)) * g


def _resident(shape):
    return pl.BlockSpec(shape, lambda i: (0,) * len(shape), pipeline_mode=pl.Buffered(1))


def _resident_layer(shape, layer):
    return pl.BlockSpec((None,) + shape, lambda i: (layer,) + (0,) * len(shape),
                        pipeline_mode=pl.Buffered(1))


def _row_spec(tm, width):
    return pl.BlockSpec((tm, width), lambda i: (i, 0))


def _params():
    return pltpu.CompilerParams(dimension_semantics=("arbitrary",),
                                vmem_limit_bytes=VMEM_LIMIT_BYTES)


def _cast_plan(weights, n_steps):
    in_specs, out_specs, out_shapes = [], [], []
    for w, layer in weights:
        _, rows, cols = w.shape
        block_rows = next(r for r in range(BF16_SUBLANES, rows + 1, BF16_SUBLANES)
                          if rows % r == 0 and rows // r <= n_steps)
        last = rows // block_rows - 1
        in_specs.append(pl.BlockSpec((None, block_rows, cols),
                                     lambda i, layer=layer, last=last: (layer, jnp.minimum(i, last), 0)))
        out_specs.append(pl.BlockSpec((block_rows, cols),
                                      lambda i, last=last: (jnp.minimum(i, last), 0)))
        out_shapes.append(jax.ShapeDtypeStruct((rows, cols), BF16))
    return in_specs, out_specs, out_shapes


def _cast_blocks(src_refs, dst_refs):
    for src, dst in zip(src_refs, dst_refs):
        dst[...] = src[...].astype(BF16)


def _ffn_kernel(*refs, fc, final_norm, n_cast):
    n_in = 5 if final_norm else 4
    x_ref, g_ref, win_ref, wo_ref = refs[:4]
    gf_ref = refs[4] if final_norm else None
    cast_src = refs[n_in:n_in + n_cast]
    o_ref = refs[n_in + n_cast]
    cast_dst = refs[n_in + n_cast + 1:n_in + 2 * n_cast + 1]
    xn_ref, h_ref = refs[n_in + 2 * n_cast + 1:]
    tm = x_ref.shape[0]
    f = wo_ref.shape[0]

    def hidden(rows, lo, hi):
        gate = jnp.dot(xn_ref[rows, :], win_ref[:, lo:hi], preferred_element_type=F32)
        up = jnp.dot(xn_ref[rows, :], win_ref[:, f + lo:f + hi], preferred_element_type=F32)
        h_ref[rows, lo:hi] = ((gate / (1.0 + jnp.exp(-gate))) * up).astype(BF16)

    part = tm // NORM_PARTS
    for r in range(0, tm, part):
        rows = slice(r, r + part)
        xn_ref[rows, :] = _rmsnorm(x_ref[rows, :], g_ref[...]).astype(BF16)
        hidden(rows, 0, fc)
    for lo in range(fc, f, fc):
        hidden(slice(None), lo, min(lo + fc, f))
    y = jnp.dot(h_ref[...], wo_ref[...], preferred_element_type=F32)
    out = x_ref[...] + 0.5 * y
    if final_norm:
        out = _rmsnorm(out, gf_ref[...])
    o_ref[...] = out
    _cast_blocks(cast_src, cast_dst)


def _ffn(x, gain, w_in, w_out, cast_next, final_gain=None):
    n, d = x.shape
    f = w_out.shape[0]
    fc = FFN_CHUNK
    tm = FFN_ROW_TILE
    assert f % MXU_DIM == 0 and fc % MXU_DIM == 0 and n % tm == 0
    final_norm = final_gain is not None
    in_specs = [_row_spec(tm, d), _resident((1, d)), _resident((d, 2 * f)), _resident((f, d))]
    args = [x, gain.reshape(1, d), w_in, w_out]
    if final_norm:
        in_specs.append(_resident((1, d)))
        args.append(final_gain.reshape(1, d))
    cast_in, cast_out, cast_shapes = _cast_plan(cast_next, n // tm)
    out, *casts = pl.pallas_call(
        functools.partial(_ffn_kernel, fc=fc, final_norm=final_norm, n_cast=len(cast_next)),
        grid=(n // tm,),
        in_specs=in_specs + cast_in,
        out_specs=[_row_spec(tm, d)] + cast_out,
        out_shape=[jax.ShapeDtypeStruct((n, d), F32)] + cast_shapes,
        scratch_shapes=[pltpu.VMEM((tm, d), BF16), pltpu.VMEM((tm, f), BF16)],
        compiler_params=_params(),
        name="ffn_final" if final_norm else "ffn",
    )(*args, *(w for w, _ in cast_next))
    return out, casts


def _conv_kernel(*refs, n_chunks, cc, tiles_per_seq, n_cast):
    x_ref, g_ref, win_ref, ck_ref, wout_ref = refs[:5]
    cast_src = refs[5:5 + n_cast]
    o_ref = refs[5 + n_cast]
    cast_dst = refs[6 + n_cast:6 + 2 * n_cast]
    xn_ref, hc_ref, carry_ref = refs[6 + 2 * n_cast:]
    tm = x_ref.shape[0]
    i = pl.program_id(0)

    @pl.when(i % tiles_per_seq == 0)
    def _():
        carry_ref[...] = jnp.zeros_like(carry_ref)

    d = n_chunks * cc
    xn_ref[...] = _rmsnorm(x_ref[...], g_ref[...]).astype(BF16)
    row = lax.broadcasted_iota(jnp.int32, (tm, cc), 0)
    for j in range(n_chunks):
        cols = slice(j * cc, (j + 1) * cc)
        gate_b, gate_c, u = (
            jnp.dot(xn_ref[...], win_ref[:, part * d + j * cc:part * d + (j + 1) * cc],
                    preferred_element_type=F32) for part in range(3))
        z = gate_c * u
        prev = carry_ref[j]
        carry_ref[j] = z[tm - 8:, :]
        p1 = jnp.broadcast_to(prev[7:8, :], (tm, cc))
        p2 = jnp.broadcast_to(prev[6:7, :], (tm, cc))
        z1 = jnp.where(row == 0, p1, pltpu.roll(z, 1, 0))
        z2 = jnp.where(row == 0, p2, jnp.where(row == 1, p1, pltpu.roll(z, 2, 0)))
        k = ck_ref[:, cols]
        conv = k[0:1, :] * z2
        conv = conv + k[1:2, :] * z1
        conv = conv + k[2:3, :] * z
        hc_ref[:, cols] = (gate_b * conv).astype(BF16)
    y = jnp.dot(hc_ref[...], wout_ref[...], preferred_element_type=F32)
    o_ref[...] = x_ref[...] + y
    _cast_blocks(cast_src, cast_dst)


def _conv_mixer(x, gain, w_in, conv_k, w_out, layer, seq_len, cast_next):
    n, d = x.shape
    tm = CONV_ROW_TILE
    cc = MXU_DIM
    n_chunks = d // cc
    assert seq_len % tm == 0 and n % seq_len == 0 and CONV_WIDTH - 1 <= 8
    cast_in, cast_out, cast_shapes = _cast_plan(cast_next, n // tm)
    out, *casts = pl.pallas_call(
        functools.partial(_conv_kernel, n_chunks=n_chunks, cc=cc, tiles_per_seq=seq_len // tm,
                          n_cast=len(cast_next)),
        grid=(n // tm,),
        in_specs=[_row_spec(tm, d), _resident((1, d)), _resident((d, 3 * d)),
                  _resident_layer((CONV_WIDTH, d), layer), _resident((d, d))] + cast_in,
        out_specs=[_row_spec(tm, d)] + cast_out,
        out_shape=[jax.ShapeDtypeStruct((n, d), F32)] + cast_shapes,
        scratch_shapes=[pltpu.VMEM((tm, d), BF16), pltpu.VMEM((tm, d), BF16),
                        pltpu.VMEM((n_chunks, 8, cc), F32)],
        compiler_params=_params(),
        name="conv_mixer",
    )(x, gain.reshape(1, d), w_in, conv_k, w_out, *(w for w, _ in cast_next))
    return out, casts


def _rope_kernel(*refs, n_cast):
    pos_ref, invf_ref = refs[:2]
    cast_src = refs[2:2 + n_cast]
    cos_ref, sin_ref = refs[2 + n_cast:4 + n_cast]
    cast_dst = refs[4 + n_cast:]
    _cast_blocks(cast_src, cast_dst)
    n_freq = HEAD_DIM // 2
    ang = pos_ref[...].astype(F32) * invf_ref[...]
    lane = lax.broadcasted_iota(jnp.int32, ang.shape, 1)
    sign = jnp.where((lane & n_freq) == 0, -1.0, 1.0)
    groups = LANES // n_freq
    group = lax.shift_right_logical(lane, n_freq.bit_length() - 1)
    for dense, out_ref, factor in ((jnp.cos(ang), cos_ref, None), (jnp.sin(ang), sin_ref, sign)):
        rolled = [dense] + [pltpu.roll(dense, s * n_freq, 1) for s in range(1, groups)]
        for q in range(groups):
            t = rolled[(groups - 1 - q) % groups]
            for g in range(groups - 1):
                t = jnp.where(group == g, rolled[(g - q) % groups], t)
            out_ref[q] = t if factor is None else t * factor


def _rope_tables(positions_flat, cast_next):
    n = positions_flat.shape[0]
    n_freq = HEAD_DIM // 2
    groups = LANES // n_freq
    rows = n // groups
    tm = min(ROW_TILE, rows)
    assert n % groups == 0 and rows % tm == 0
    inv_freq = ROPE_THETA ** (-jnp.arange(0, HEAD_DIM, 2, dtype=F32) / HEAD_DIM)
    invf = jnp.tile(inv_freq, groups).reshape(1, LANES)
    pos = jnp.repeat(positions_flat.reshape(groups, rows).T, n_freq, axis=1)
    out_spec = pl.BlockSpec((groups, tm, LANES), lambda i: (0, i, 0))
    cast_in, cast_out, cast_shapes = _cast_plan(cast_next, rows // tm)
    cos_t, sin_t, *casts = pl.pallas_call(
        functools.partial(_rope_kernel, n_cast=len(cast_next)),
        grid=(rows // tm,),
        in_specs=[_row_spec(tm, LANES), _resident((1, LANES))] + cast_in,
        out_specs=[out_spec, out_spec] + cast_out,
        out_shape=[jax.ShapeDtypeStruct((groups, rows, LANES), F32)] * 2 + cast_shapes,
        compiler_params=_params(),
        name="rope_tables",
    )(pos, invf, *(w for w, _ in cast_next))
    return cos_t.reshape(n, LANES), sin_t.reshape(n, LANES), casts


def _transpose_blocks(t):
    return jnp.concatenate([t[r:r + WINDOW, :].T for r in range(0, t.shape[0], WINDOW)], axis=1)


def _attn_kernel(*refs, tiles_per_seq, n_parts, n_cast):
    sinks_ref, x_ref, g_ref, cos_ref, sin_ref, wqkv_ref, wo_ref = refs[:7]
    cast_src = refs[7:7 + n_cast]
    o_ref = refs[7 + n_cast]
    cast_dst = refs[8 + n_cast:8 + 2 * n_cast]
    xn_ref, q_ref, ka_ref, kb_ref, vat_ref, vbt_ref, att_ref = refs[8 + 2 * n_cast:]
    tm, d = x_ref.shape
    part = tm // n_parts
    blocks_per_part = part // WINDOW
    n_q_slabs = d // LANES
    n_kv_slabs = N_KV_HEADS * HEAD_DIM // LANES
    i = pl.program_id(0)
    first = i % tiles_per_seq == 0

    @pl.when(first)
    def _():
        for r in (ka_ref, kb_ref):
            r[:, 0:WINDOW, :] = jnp.zeros((N_KV_HEADS, WINDOW, LANES), BF16)
        for r in (vat_ref, vbt_ref):
            r[:, 0:LANES, 0:WINDOW] = jnp.zeros((N_KV_HEADS, LANES, WINDOW), BF16)

    @pl.when(jnp.logical_not(first))
    def _():
        for r in (ka_ref, kb_ref):
            r[:, 0:WINDOW, :] = r[:, tm:tm + WINDOW, :]
        for r in (vat_ref, vbt_ref):
            r[:, 0:LANES, 0:WINDOW] = r[:, 0:LANES, tm:tm + WINDOW]

    extra = lax.broadcasted_iota(jnp.int32, (BF16_SUBLANES, tm + WINDOW), 0)
    for r, row in ((vat_ref, 0), (vbt_ref, 1)):
        ones_row = jnp.where(extra == row, 1.0, 0.0).astype(BF16)
        for head in range(N_KV_HEADS):
            r[head, LANES:, :] = ones_row

    lane = lax.broadcasted_iota(jnp.int32, (part, LANES), 1)
    first_half = (lane & (HEAD_DIM // 2)) == 0
    low_head = lane < HEAD_DIM
    zero = jnp.zeros((part, LANES), F32)
    zero_t = jnp.zeros((HEAD_DIM, part), BF16)
    q_scale = LOG2_E / math.sqrt(HEAD_DIM)

    def rope(t, rows):
        partner = jnp.where(first_half,
                            pltpu.roll(t, LANES - HEAD_DIM // 2, 1),
                            pltpu.roll(t, HEAD_DIM // 2, 1))
        return t * cos_ref[rows, :] + partner * sin_ref[rows, :]

    def project_chunk(p, c):
        rows = slice(p * part, (p + 1) * part)
        halo_rows = slice(WINDOW + p * part, WINDOW + (p + 1) * part)
        if c == 0:
            xn_ref[rows, :] = _rmsnorm(x_ref[rows, :], g_ref[...]).astype(BF16)
        t2 = jnp.dot(xn_ref[rows, :], wqkv_ref[:, c * MXU_DIM:(c + 1) * MXU_DIM],
                     preferred_element_type=F32)
        for half in range(MXU_DIM // LANES):
            s = c * (MXU_DIM // LANES) + half
            t = t2[:, half * LANES:(half + 1) * LANES]
            if s < n_q_slabs:
                q_ref[rows, s * LANES:(s + 1) * LANES] = (rope(t, rows) * q_scale).astype(BF16)
            elif s < n_q_slabs + n_kv_slabs:
                ks = s - n_q_slabs
                k_slab = rope(t, rows)
                swapped = pltpu.roll(k_slab, HEAD_DIM, 1)
                ka_ref[2 * ks, halo_rows, :] = jnp.where(low_head, k_slab, zero).astype(BF16)
                kb_ref[2 * ks, halo_rows, :] = jnp.where(low_head, zero, swapped).astype(BF16)
                ka_ref[2 * ks + 1, halo_rows, :] = jnp.where(low_head, swapped, zero).astype(BF16)
                kb_ref[2 * ks + 1, halo_rows, :] = jnp.where(low_head, zero, k_slab).astype(BF16)
            else:
                vs = s - n_q_slabs - n_kv_slabs
                v_t = _transpose_blocks(t).astype(BF16)
                for head, v_head in ((2 * vs, v_t[:HEAD_DIM, :]), (2 * vs + 1, v_t[HEAD_DIM:, :])):
                    vat_ref[head, 0:LANES, halo_rows] = jnp.concatenate([v_head, zero_t], axis=0)
                    vbt_ref[head, 0:LANES, halo_rows] = jnp.concatenate([zero_t, v_head], axis=0)

    def out_chunk(p, c):
        rows = slice(p * part, (p + 1) * part)
        cols = slice(c * MXU_DIM, (c + 1) * MXU_DIM)
        y = jnp.dot(att_ref[rows, :], wo_ref[:, cols], preferred_element_type=F32)
        o_ref[rows, cols] = x_ref[rows, cols] + y

    project_chunks = wqkv_ref.shape[1] // MXU_DIM
    out_chunks = d // MXU_DIM

    HALF = WINDOW // 2
    kj = lax.broadcasted_iota(jnp.int32, (WINDOW, LANES), 0)
    ql = lax.broadcasted_iota(jnp.int32, (WINDOW, LANES), 1) & (HALF - 1)
    pair_of_lane = lax.broadcasted_iota(jnp.int32, (1, LANES), 1) < HALF
    neg = jnp.finfo(F32).min
    no_prev = jnp.where(first, WINDOW, 0)

    def band_mask(h, seq_first_block):
        qi = ql + h * HALF
        prev_ok = kj > (qi + no_prev if seq_first_block else qi)
        cur_ok = kj <= qi
        if h == 0:
            return slice(0, WINDOW + HALF), jnp.concatenate([prev_ok, cur_ok[:HALF]], axis=0)
        return slice(HALF, 2 * WINDOW), jnp.concatenate([prev_ok[HALF:], cur_ok], axis=0)

    def scores(b, g):
        keys = slice(b * WINDOW, b * WINDOW + 2 * WINDOW)
        q2 = jnp.concatenate(
            [q_ref[b * WINDOW + h * HALF:b * WINDOW + (h + 1) * HALF, sl * LANES:(sl + 1) * LANES]
             for h in range(2) for sl in (2 * g, 2 * g + 1)], axis=0)
        k2 = jnp.concatenate([ka_ref[g, keys, :], kb_ref[g, keys, :]], axis=0)
        return lax.dot_general(k2, q2, (((1,), (1,)), ((), ())),
                               preferred_element_type=F32)

    def attend(b, g, s_t):
        keys = slice(b * WINDOW, b * WINDOW + 2 * WINDOW)
        p_cols = []
        sink_terms = []
        for h in range(2):
            active, mask = band_mask(h, b == 0)
            p_col = []
            sink_col = []
            for lo in range(2):
                sink = jnp.where(pair_of_lane, sinks_ref[GROUP * g + lo],
                                 sinks_ref[GROUP * g + 2 + lo]) * LOG2_E
                sc = s_t[lo * 2 * WINDOW + active.start:lo * 2 * WINDOW + active.stop,
                         h * LANES:(h + 1) * LANES]
                sc = jnp.where(mask, sc, neg)
                m = jnp.maximum(jnp.max(sc, axis=0, keepdims=True), sink)
                p = jnp.exp2(sc - m).astype(BF16)
                skipped = jnp.zeros((HALF, LANES), BF16)
                p_col.append(jnp.concatenate([p, skipped] if h == 0 else [skipped, p], axis=0))
                sink_col.append(jnp.exp2(sink - m))
            p_cols.append(jnp.concatenate(p_col, axis=0))
            sink_terms.append(sink_col)
        p_t = jnp.concatenate(p_cols, axis=1)
        v2_t = jnp.concatenate([vat_ref[g, :, keys], vbt_ref[g, :, keys]], axis=1)
        o_t = jnp.dot(v2_t, p_t, preferred_element_type=F32)
        for h in range(2):
            cols = slice(h * LANES, (h + 1) * LANES)
            inv = [1.0 / (o_t[LANES + lo:LANES + lo + 1, cols] + sink_terms[h][lo])
                   for lo in range(2)]
            o_half = jnp.concatenate([o_t[:HEAD_DIM, cols] * inv[0],
                                      o_t[HEAD_DIM:LANES, cols] * inv[1]], axis=0)
            o_half = o_half.T.astype(BF16)
            rows = slice(b * WINDOW + h * HALF, b * WINDOW + (h + 1) * HALF)
            for pair in range(2):
                slab = 2 * g + pair
                att_ref[rows, slab * LANES:(slab + 1) * LANES] = (
                    o_half[pair * HALF:(pair + 1) * HALF, :])

    units = [(b, g) for b in range(tm // WINDOW) for g in range(N_KV_HEADS)]
    units_per_part = blocks_per_part * N_KV_HEADS
    filler_slots = units_per_part - SCORE_LOOKAHEAD

    for c in range(project_chunks):
        project_chunk(0, c)
    pending = [scores(*u) for u in units[:SCORE_LOOKAHEAD]]
    fillers = []
    for n, (b, g) in enumerate(units):
        p, j = divmod(n, units_per_part)
        if j == 0:
            fillers = []
            if p + 1 < n_parts:
                fillers += [functools.partial(project_chunk, p + 1, c)
                            for c in range(project_chunks)]
            if p >= 1:
                fillers += [functools.partial(out_chunk, p - 1, c) for c in range(out_chunks)]
        if n + SCORE_LOOKAHEAD < len(units):
            pending.append(scores(*units[n + SCORE_LOOKAHEAD]))
        attend(b, g, pending[n])
        pending[n] = None
        if j < filler_slots:
            lo_f = len(fillers) * j // filler_slots
            hi_f = len(fillers) * (j + 1) // filler_slots
            for filler in fillers[lo_f:hi_f]:
                filler()
    for c in range(out_chunks):
        out_chunk(n_parts - 1, c)
    _cast_blocks(cast_src, cast_dst)


def _attention(x, gain, cos_t, sin_t, w_qkv, sinks, w_o, seq_len, cast_next):
    n, d = x.shape
    tm = ATTN_ROW_TILE
    n_parts = ATTN_PARTS
    qkv_w = w_qkv.shape[1]
    assert seq_len % tm == 0 and tm % (n_parts * WINDOW) == 0 and d == N_Q_HEADS * HEAD_DIM
    assert qkv_w % MXU_DIM == 0 and (tm // n_parts // WINDOW) * N_KV_HEADS > SCORE_LOOKAHEAD
    k_halo = pltpu.VMEM((N_KV_HEADS, tm + WINDOW, LANES), BF16)
    vt_halo = pltpu.VMEM((N_KV_HEADS, LANES + BF16_SUBLANES, tm + WINDOW), BF16)
    cast_in, cast_out, cast_shapes = _cast_plan(cast_next, n // tm)
    out, *casts = pl.pallas_call(
        functools.partial(_attn_kernel, tiles_per_seq=seq_len // tm, n_parts=n_parts,
                          n_cast=len(cast_next)),
        grid=(n // tm,),
        in_specs=[pl.BlockSpec(memory_space=pltpu.SMEM),
                  _row_spec(tm, d), _resident((1, d)), _row_spec(tm, LANES), _row_spec(tm, LANES),
                  _resident((d, qkv_w)), _resident((d, d))] + cast_in,
        out_specs=[_row_spec(tm, d)] + cast_out,
        out_shape=[jax.ShapeDtypeStruct((n, d), F32)] + cast_shapes,
        scratch_shapes=[pltpu.VMEM((tm, d), BF16), pltpu.VMEM((tm, d), BF16),
                        k_halo, k_halo, vt_halo, vt_halo,
                        pltpu.VMEM((tm, d), BF16)],
        compiler_params=_params(),
        name="swa_attention",
    )(sinks, x, gain.reshape(1, d), cos_t, sin_t, w_qkv, w_o, *(w for w, _ in cast_next))
    return out, casts


def kernel(x, positions, ln_ff1, w_ff1_in, w_ff1_out, ln_mix, ln_ff2, w_ff2_in, w_ff2_out,
           conv_w_in, conv_kernel, conv_w_out, attn_w_qkv, attn_sinks, attn_w_o, ln_final):
    batch, seq_len, d = x.shape
    depth = ln_ff1.shape[0]
    h = x.reshape(batch * seq_len, d)

    stages = []
    for i in range(depth):
        j = i // 2
        stages.append(("ffn", [(w_ff1_in, i), (w_ff1_out, i)],
                       dict(gain=ln_ff1[i], final_gain=None)))
        if i % 2 == 0:
            stages.append(("conv", [(conv_w_in, j), (conv_w_out, j)],
                           dict(gain=ln_mix[i], layer=j)))
        else:
            stages.append(("attn", [(attn_w_qkv, j), (attn_w_o, j)],
                           dict(gain=ln_mix[i], sinks=attn_sinks[j])))
        stages.append(("ffn", [(w_ff2_in, i), (w_ff2_out, i)],
                       dict(gain=ln_ff2[i], final_gain=ln_final if i == depth - 1 else None)))

    cos_t, sin_t, weights = _rope_tables(positions.reshape(batch * seq_len), stages[0][1])
    for k, (kind, _, args) in enumerate(stages):
        cast_next = stages[k + 1][1] if k + 1 < len(stages) else []
        if kind == "ffn":
            h, weights = _ffn(h, args["gain"], *weights, cast_next,
                              final_gain=args["final_gain"])
        elif kind == "conv":
            h, weights = _conv_mixer(h, args["gain"], weights[0], conv_kernel, weights[1],
                                     args["layer"], seq_len, cast_next)
        else:
            h, weights = _attention(h, args["gain"], cos_t, sin_t, weights[0], args["sinks"],
                                    weights[1], seq_len, cast_next)
    return h.reshape(batch, seq_len, d)
```

```python
import functools
import math

import jax
import jax.numpy as jnp
from jax import lax
from jax.experimental import pallas as pl
from jax.experimental.pallas import tpu as pltpu

RMS_EPS = 1e-6
ROPE_THETA = 10000.0
HEAD_DIM = 64
N_Q_HEADS = 16
N_KV_HEADS = 4
GROUP = N_Q_HEADS // N_KV_HEADS
WINDOW = 128
CONV_WIDTH = 3
LANES = 128
BF16_SUBLANES = 16
MXU_DIM = 256

ROW_TILE = 512
CONV_ROW_TILE = 1024
FFN_ROW_TILE = 1024
FFN_CHUNK = 256
NORM_PARTS = 4
ATTN_ROW_TILE = 1024
ATTN_PARTS = 4
SCORE_LOOKAHEAD = 2
LOG2_E = math.log2(math.e)
VMEM_LIMIT_BYTES = 56 * 1024 * 1024

F32 = jnp.float32
BF16 = jnp.bfloat16


def _rmsnorm(x, g):
    ms = jnp.mean(x * x, axis=-1, keepdims=True)
    return (x * lax.rsqrt(ms + RMS_EPS)) * g


def _resident(shape):
    return pl.BlockSpec(shape, lambda i: (0,) * len(shape), pipeline_mode=pl.Buffered(1))


def _resident_layer(shape, layer):
    return pl.BlockSpec((None,) + shape, lambda i: (layer,) + (0,) * len(shape),
                        pipeline_mode=pl.Buffered(1))


def _row_spec(tm, width):
    return pl.BlockSpec((tm, width), lambda i: (i, 0))


def _params():
    return pltpu.CompilerParams(dimension_semantics=("arbitrary",),
                                vmem_limit_bytes=VMEM_LIMIT_BYTES)


def _cast_plan(weights, n_steps):
    in_specs, out_specs, out_shapes = [], [], []
    for w, layer in weights:
        _, rows, cols = w.shape
        block_rows = next(r for r in range(BF16_SUBLANES, rows + 1, BF16_SUBLANES)
                          if rows % r == 0 and rows // r <= n_steps)
        last = rows // block_rows - 1
        in_specs.append(pl.BlockSpec((None, block_rows, cols),
                                     lambda i, layer=layer, last=last: (layer, jnp.minimum(i, last), 0)))
        out_specs.append(pl.BlockSpec((block_rows, cols),
                                      lambda i, last=last: (jnp.minimum(i, last), 0)))
        out_shapes.append(jax.ShapeDtypeStruct((rows, cols), BF16))
    return in_specs, out_specs, out_shapes


def _cast_blocks(src_refs, dst_refs):
    for src, dst in zip(src_refs, dst_refs):
        dst[...] = src[...].astype(BF16)


def _ffn_kernel(*refs, fc, final_norm, n_cast):
    n_in = 5 if final_norm else 4
    x_ref, g_ref, win_ref, wo_ref = refs[:4]
    gf_ref = refs[4] if final_norm else None
    cast_src = refs[n_in:n_in + n_cast]
    o_ref = refs[n_in + n_cast]
    cast_dst = refs[n_in + n_cast + 1:n_in + 2 * n_cast + 1]
    xn_ref, h_ref = refs[n_in + 2 * n_cast + 1:]
    tm = x_ref.shape[0]
    f = wo_ref.shape[0]

    def hidden(rows, lo, hi):
        gate = jnp.dot(xn_ref[rows, :], win_ref[:, lo:hi], preferred_element_type=F32)
        up = jnp.dot(xn_ref[rows, :], win_ref[:, f + lo:f + hi], preferred_element_type=F32)
        h_ref[rows, lo:hi] = ((gate / (1.0 + jnp.exp(-gate))) * up).astype(BF16)

    part = tm // NORM_PARTS
    for r in range(0, tm, part):
        rows = slice(r, r + part)
        xn_ref[rows, :] = _rmsnorm(x_ref[rows, :], g_ref[...]).astype(BF16)
        hidden(rows, 0, fc)
    for lo in range(fc, f, fc):
        hidden(slice(None), lo, min(lo + fc, f))
    y = jnp.dot(h_ref[...], wo_ref[...], preferred_element_type=F32)
    out = x_ref[...] + 0.5 * y
    if final_norm:
        out = _rmsnorm(out, gf_ref[...])
    o_ref[...] = out
    _cast_blocks(cast_src, cast_dst)


def _ffn(x, gain, w_in, w_out, cast_next, final_gain=None):
    n, d = x.shape
    f = w_out.shape[0]
    fc = FFN_CHUNK
    tm = FFN_ROW_TILE
    assert f % MXU_DIM == 0 and fc % MXU_DIM == 0 and n % tm == 0
    final_norm = final_gain is not None
    in_specs = [_row_spec(tm, d), _resident((1, d)), _resident((d, 2 * f)), _resident((f, d))]
    args = [x, gain.reshape(1, d), w_in, w_out]
    if final_norm:
        in_specs.append(_resident((1, d)))
        args.append(final_gain.reshape(1, d))
    cast_in, cast_out, cast_shapes = _cast_plan(cast_next, n // tm)
    out, *casts = pl.pallas_call(
        functools.partial(_ffn_kernel, fc=fc, final_norm=final_norm, n_cast=len(cast_next)),
        grid=(n // tm,),
        in_specs=in_specs + cast_in,
        out_specs=[_row_spec(tm, d)] + cast_out,
        out_shape=[jax.ShapeDtypeStruct((n, d), F32)] + cast_shapes,
        scratch_shapes=[pltpu.VMEM((tm, d), BF16), pltpu.VMEM((tm, f), BF16)],
        compiler_params=_params(),
        name="ffn_final" if final_norm else "ffn",
    )(*args, *(w for w, _ in cast_next))
    return out, casts


def _conv_kernel(*refs, n_chunks, cc, tiles_per_seq, n_cast):
    x_ref, g_ref, win_ref, ck_ref, wout_ref = refs[:5]
    cast_src = refs[5:5 + n_cast]
    o_ref = refs[5 + n_cast]
    cast_dst = refs[6 + n_cast:6 + 2 * n_cast]
    xn_ref, hc_ref, carry_ref = refs[6 + 2 * n_cast:]
    tm = x_ref.shape[0]
    i = pl.program_id(0)

    @pl.when(i % tiles_per_seq == 0)
    def _():
        carry_ref[...] = jnp.zeros_like(carry_ref)

    d = n_chunks * cc
    xn_ref[...] = _rmsnorm(x_ref[...], g_ref[...]).astype(BF16)
    row = lax.broadcasted_iota(jnp.int32, (tm, cc), 0)
    for j in range(n_chunks):
        cols = slice(j * cc, (j + 1) * cc)
        gate_b, gate_c, u = (
            jnp.dot(xn_ref[...], win_ref[:, part * d + j * cc:part * d + (j + 1) * cc],
                    preferred_element_type=F32) for part in range(3))
        z = gate_c * u
        prev = carry_ref[j]
        carry_ref[j] = z[tm - 8:, :]
        p1 = jnp.broadcast_to(prev[7:8, :], (tm, cc))
        p2 = jnp.broadcast_to(prev[6:7, :], (tm, cc))
        z1 = jnp.where(row == 0, p1, pltpu.roll(z, 1, 0))
        z2 = jnp.where(row == 0, p2, jnp.where(row == 1, p1, pltpu.roll(z, 2, 0)))
        k = ck_ref[:, cols]
        conv = k[0:1, :] * z2
        conv = conv + k[1:2, :] * z1
        conv = conv + k[2:3, :] * z
        hc_ref[:, cols] = (gate_b * conv).astype(BF16)
    y = jnp.dot(hc_ref[...], wout_ref[...], preferred_element_type=F32)
    o_ref[...] = x_ref[...] + y
    _cast_blocks(cast_src, cast_dst)


def _conv_mixer(x, gain, w_in, conv_k, w_out, layer, seq_len, cast_next):
    n, d = x.shape
    tm = CONV_ROW_TILE
    cc = MXU_DIM
    n_chunks = d // cc
    assert seq_len % tm == 0 and n % seq_len == 0 and CONV_WIDTH - 1 <= 8
    cast_in, cast_out, cast_shapes = _cast_plan(cast_next, n // tm)
    out, *casts = pl.pallas_call(
        functools.partial(_conv_kernel, n_chunks=n_chunks, cc=cc, tiles_per_seq=seq_len // tm,
                          n_cast=len(cast_next)),
        grid=(n // tm,),
        in_specs=[_row_spec(tm, d), _resident((1, d)), _resident((d, 3 * d)),
                  _resident_layer((CONV_WIDTH, d), layer), _resident((d, d))] + cast_in,
        out_specs=[_row_spec(tm, d)] + cast_out,
        out_shape=[jax.ShapeDtypeStruct((n, d), F32)] + cast_shapes,
        scratch_shapes=[pltpu.VMEM((tm, d), BF16), pltpu.VMEM((tm, d), BF16),
                        pltpu.VMEM((n_chunks, 8, cc), F32)],
        compiler_params=_params(),
        name="conv_mixer",
    )(x, gain.reshape(1, d), w_in, conv_k, w_out, *(w for w, _ in cast_next))
    return out, casts


def _rope_kernel(*refs, n_cast):
    pos_ref, invf_ref = refs[:2]
    cast_src = refs[2:2 + n_cast]
    cos_ref, sin_ref = refs[2 + n_cast:4 + n_cast]
    cast_dst = refs[4 + n_cast:]
    _cast_blocks(cast_src, cast_dst)
    n_freq = HEAD_DIM // 2
    ang = pos_ref[...].astype(F32) * invf_ref[...]
    lane = lax.broadcasted_iota(jnp.int32, ang.shape, 1)
    sign = jnp.where((lane & n_freq) == 0, -1.0, 1.0)
    groups = LANES // n_freq
    group = lax.shift_right_logical(lane, n_freq.bit_length() - 1)
    for dense, out_ref, factor in ((jnp.cos(ang), cos_ref, None), (jnp.sin(ang), sin_ref, sign)):
        rolled = [dense] + [pltpu.roll(dense, s * n_freq, 1) for s in range(1, groups)]
        for q in range(groups):
            t = rolled[(groups - 1 - q) % groups]
            for g in range(groups - 1):
                t = jnp.where(group == g, rolled[(g - q) % groups], t)
            out_ref[q] = t if factor is None else t * factor


def _rope_tables(positions_flat, cast_next):
    n = positions_flat.shape[0]
    n_freq = HEAD_DIM // 2
    groups = LANES // n_freq
    rows = n // groups
    tm = min(ROW_TILE, rows)
    assert n % groups == 0 and rows % tm == 0
    inv_freq = ROPE_THETA ** (-jnp.arange(0, HEAD_DIM, 2, dtype=F32) / HEAD_DIM)
    invf = jnp.tile(inv_freq, groups).reshape(1, LANES)
    pos = jnp.repeat(positions_flat.reshape(groups, rows).T, n_freq, axis=1)
    out_spec = pl.BlockSpec((groups, tm, LANES), lambda i: (0, i, 0))
    cast_in, cast_out, cast_shapes = _cast_plan(cast_next, rows // tm)
    cos_t, sin_t, *casts = pl.pallas_call(
        functools.partial(_rope_kernel, n_cast=len(cast_next)),
        grid=(rows // tm,),
        in_specs=[_row_spec(tm, LANES), _resident((1, LANES))] + cast_in,
        out_specs=[out_spec, out_spec] + cast_out,
        out_shape=[jax.ShapeDtypeStruct((groups, rows, LANES), F32)] * 2 + cast_shapes,
        compiler_params=_params(),
        name="rope_tables",
    )(pos, invf, *(w for w, _ in cast_next))
    return cos_t.reshape(n, LANES), sin_t.reshape(n, LANES), casts


def _transpose_blocks(t):
    return jnp.concatenate([t[r:r + WINDOW, :].T for r in range(0, t.shape[0], WINDOW)], axis=1)


def _attn_kernel(*refs, tiles_per_seq, n_parts, n_cast):
    sinks_ref, x_ref, g_ref, cos_ref, sin_ref, wqkv_ref, wo_ref = refs[:7]
    cast_src = refs[7:7 + n_cast]
    o_ref = refs[7 + n_cast]
    cast_dst = refs[8 + n_cast:8 + 2 * n_cast]
    xn_ref, q_ref, ka_ref, kb_ref, vat_ref, vbt_ref, att_ref = refs[8 + 2 * n_cast:]
    tm, d = x_ref.shape
    part = tm // n_parts
    blocks_per_part = part // WINDOW
    n_q_slabs = d // LANES
    n_kv_slabs = N_KV_HEADS * HEAD_DIM // LANES
    i = pl.program_id(0)
    first = i % tiles_per_seq == 0

    @pl.when(first)
    def _():
        for r in (ka_ref, kb_ref):
            r[:, 0:WINDOW, :] = jnp.zeros((N_KV_HEADS, WINDOW, LANES), BF16)
        for r in (vat_ref, vbt_ref):
            r[:, :, 0:WINDOW] = jnp.zeros((N_KV_HEADS, LANES, WINDOW), BF16)

    @pl.when(jnp.logical_not(first))
    def _():
        for r in (ka_ref, kb_ref):
            r[:, 0:WINDOW, :] = r[:, tm:tm + WINDOW, :]
        for r in (vat_ref, vbt_ref):
            r[:, :, 0:WINDOW] = r[:, :, tm:tm + WINDOW]

    lane = lax.broadcasted_iota(jnp.int32, (part, LANES), 1)
    first_half = (lane & (HEAD_DIM // 2)) == 0
    low_head = lane < HEAD_DIM
    zero = jnp.zeros((part, LANES), F32)
    zero_t = jnp.zeros((HEAD_DIM, part), BF16)
    q_scale = LOG2_E / math.sqrt(HEAD_DIM)

    def rope(t, rows):
        partner = jnp.where(first_half,
                            pltpu.roll(t, LANES - HEAD_DIM // 2, 1),
                            pltpu.roll(t, HEAD_DIM // 2, 1))
        return t * cos_ref[rows, :] + partner * sin_ref[rows, :]

    def project_chunk(p, c):
        rows = slice(p * part, (p + 1) * part)
        halo_rows = slice(WINDOW + p * part, WINDOW + (p + 1) * part)
        if c == 0:
            xn_ref[rows, :] = _rmsnorm(x_ref[rows, :], g_ref[...]).astype(BF16)
        t2 = jnp.dot(xn_ref[rows, :], wqkv_ref[:, c * MXU_DIM:(c + 1) * MXU_DIM],
                     preferred_element_type=F32)
        for half in range(MXU_DIM // LANES):
            s = c * (MXU_DIM // LANES) + half
            t = t2[:, half * LANES:(half + 1) * LANES]
            if s < n_q_slabs:
                q_ref[rows, s * LANES:(s + 1) * LANES] = (rope(t, rows) * q_scale).astype(BF16)
            elif s < n_q_slabs + n_kv_slabs:
                ks = s - n_q_slabs
                k_slab = rope(t, rows)
                swapped = pltpu.roll(k_slab, HEAD_DIM, 1)
                ka_ref[2 * ks, halo_rows, :] = jnp.where(low_head, k_slab, zero).astype(BF16)
                kb_ref[2 * ks, halo_rows, :] = jnp.where(low_head, zero, swapped).astype(BF16)
                ka_ref[2 * ks + 1, halo_rows, :] = jnp.where(low_head, swapped, zero).astype(BF16)
                kb_ref[2 * ks + 1, halo_rows, :] = jnp.where(low_head, zero, k_slab).astype(BF16)
            else:
                vs = s - n_q_slabs - n_kv_slabs
                v_t = _transpose_blocks(t).astype(BF16)
                for head, v_head in ((2 * vs, v_t[:HEAD_DIM, :]), (2 * vs + 1, v_t[HEAD_DIM:, :])):
                    vat_ref[head, :, halo_rows] = jnp.concatenate([v_head, zero_t], axis=0)
                    vbt_ref[head, :, halo_rows] = jnp.concatenate([zero_t, v_head], axis=0)

    def out_chunk(p, c):
        rows = slice(p * part, (p + 1) * part)
        cols = slice(c * MXU_DIM, (c + 1) * MXU_DIM)
        y = jnp.dot(att_ref[rows, :], wo_ref[:, cols], preferred_element_type=F32)
        o_ref[rows, cols] = x_ref[rows, cols] + y

    project_chunks = wqkv_ref.shape[1] // MXU_DIM
    out_chunks = d // MXU_DIM

    HALF = WINDOW // 2
    kj = lax.broadcasted_iota(jnp.int32, (WINDOW, LANES), 0)
    ql = lax.broadcasted_iota(jnp.int32, (WINDOW, LANES), 1) & (HALF - 1)
    pair_of_lane = lax.broadcasted_iota(jnp.int32, (1, LANES), 1) < HALF
    neg = jnp.finfo(F32).min
    no_prev = jnp.where(first, WINDOW, 0)

    def band_mask(h, seq_first_block):
        qi = ql + h * HALF
        prev_ok = kj > (qi + no_prev if seq_first_block else qi)
        cur_ok = kj <= qi
        if h == 0:
            return slice(0, WINDOW + HALF), jnp.concatenate([prev_ok, cur_ok[:HALF]], axis=0)
        return slice(HALF, 2 * WINDOW), jnp.concatenate([prev_ok[HALF:], cur_ok], axis=0)

    def scores(b, g):
        keys = slice(b * WINDOW, b * WINDOW + 2 * WINDOW)
        q2 = jnp.concatenate(
            [q_ref[b * WINDOW + h * HALF:b * WINDOW + (h + 1) * HALF, sl * LANES:(sl + 1) * LANES]
             for h in range(2) for sl in (2 * g, 2 * g + 1)], axis=0)
        k2 = jnp.concatenate([ka_ref[g, keys, :], kb_ref[g, keys, :]], axis=0)
        return lax.dot_general(k2, q2, (((1,), (1,)), ((), ())),
                               preferred_element_type=F32)

    def attend(b, g, s_t):
        keys = slice(b * WINDOW, b * WINDOW + 2 * WINDOW)
        p_cols = []
        denoms = []
        for h in range(2):
            active, mask = band_mask(h, b == 0)
            p_col = []
            denom_col = []
            for lo in range(2):
                sink = jnp.where(pair_of_lane, sinks_ref[GROUP * g + lo],
                                 sinks_ref[GROUP * g + 2 + lo]) * LOG2_E
                sc = s_t[lo * 2 * WINDOW + active.start:lo * 2 * WINDOW + active.stop,
                         h * LANES:(h + 1) * LANES]
                sc = jnp.where(mask, sc, neg)
                m = jnp.maximum(jnp.max(sc, axis=0, keepdims=True), sink)
                p = jnp.exp2(sc - m)
                denom_col.append(jnp.sum(p, axis=0, keepdims=True) + jnp.exp2(sink - m))
                p = p.astype(BF16)
                skipped = jnp.zeros((HALF, LANES), BF16)
                p_col.append(jnp.concatenate([p, skipped] if h == 0 else [skipped, p], axis=0))
            p_cols.append(jnp.concatenate(p_col, axis=0))
            denoms.append(denom_col)
        p_t = jnp.concatenate(p_cols, axis=1)
        v2_t = jnp.concatenate([vat_ref[g, :, keys], vbt_ref[g, :, keys]], axis=1)
        o_t = jnp.dot(v2_t, p_t, preferred_element_type=F32)
        for h in range(2):
            cols = slice(h * LANES, (h + 1) * LANES)
            inv = [1.0 / denoms[h][lo] for lo in range(2)]
            o_half = jnp.concatenate([o_t[:HEAD_DIM, cols] * inv[0],
                                      o_t[HEAD_DIM:LANES, cols] * inv[1]], axis=0)
            o_half = o_half.T.astype(BF16)
            rows = slice(b * WINDOW + h * HALF, b * WINDOW + (h + 1) * HALF)
            for pair in range(2):
                slab = 2 * g + pair
                att_ref[rows, slab * LANES:(slab + 1) * LANES] = (
                    o_half[pair * HALF:(pair + 1) * HALF, :])

    units = [(b, g) for b in range(tm // WINDOW) for g in range(N_KV_HEADS)]
    units_per_part = blocks_per_part * N_KV_HEADS
    filler_slots = units_per_part - SCORE_LOOKAHEAD

    for c in range(project_chunks):
        project_chunk(0, c)
    pending = [scores(*u) for u in units[:SCORE_LOOKAHEAD]]
    fillers = []
    for n, (b, g) in enumerate(units):
        p, j = divmod(n, units_per_part)
        if j == 0:
            fillers = []
            if p + 1 < n_parts:
                fillers += [functools.partial(project_chunk, p + 1, c)
                            for c in range(project_chunks)]
            if p >= 1:
                fillers += [functools.partial(out_chunk, p - 1, c) for c in range(out_chunks)]
        if n + SCORE_LOOKAHEAD < len(units):
            pending.append(scores(*units[n + SCORE_LOOKAHEAD]))
        attend(b, g, pending[n])
        pending[n] = None
        if j < filler_slots:
            lo_f = len(fillers) * j // filler_slots
            hi_f = len(fillers) * (j + 1) // filler_slots
            for filler in fillers[lo_f:hi_f]:
                filler()
    for c in range(out_chunks):
        out_chunk(n_parts - 1, c)
    _cast_blocks(cast_src, cast_dst)


def _attention(x, gain, cos_t, sin_t, w_qkv, sinks, w_o, seq_len, cast_next):
    n, d = x.shape
    tm = ATTN_ROW_TILE
    n_parts = ATTN_PARTS
    qkv_w = w_qkv.shape[1]
    assert seq_len % tm == 0 and tm % (n_parts * WINDOW) == 0 and d == N_Q_HEADS * HEAD_DIM
    assert qkv_w % MXU_DIM == 0 and (tm // n_parts // WINDOW) * N_KV_HEADS > SCORE_LOOKAHEAD
    k_halo = pltpu.VMEM((N_KV_HEADS, tm + WINDOW, LANES), BF16)
    vt_halo = pltpu.VMEM((N_KV_HEADS, LANES, tm + WINDOW), BF16)
    cast_in, cast_out, cast_shapes = _cast_plan(cast_next, n // tm)
    out, *casts = pl.pallas_call(
        functools.partial(_attn_kernel, tiles_per_seq=seq_len // tm, n_parts=n_parts,
                          n_cast=len(cast_next)),
        grid=(n // tm,),
        in_specs=[pl.BlockSpec(memory_space=pltpu.SMEM),
                  _row_spec(tm, d), _resident((1, d)), _row_spec(tm, LANES), _row_spec(tm, LANES),
                  _resident((d, qkv_w)), _resident((d, d))] + cast_in,
        out_specs=[_row_spec(tm, d)] + cast_out,
        out_shape=[jax.ShapeDtypeStruct((n, d), F32)] + cast_shapes,
        scratch_shapes=[pltpu.VMEM((tm, d), BF16), pltpu.VMEM((tm, d), BF16),
                        k_halo, k_halo, vt_halo, vt_halo,
                        pltpu.VMEM((tm, d), BF16)],
        compiler_params=_params(),
        name="swa_attention",
    )(sinks, x, gain.reshape(1, d), cos_t, sin_t, w_qkv, w_o, *(w for w, _ in cast_next))
    return out, casts


def kernel(x, positions, ln_ff1, w_ff1_in, w_ff1_out, ln_mix, ln_ff2, w_ff2_in, w_ff2_out,
           conv_w_in, conv_kernel, conv_w_out, attn_w_qkv, attn_sinks, attn_w_o, ln_final):
    batch, seq_len, d = x.shape
    depth = ln_ff1.shape[0]
    h = x.reshape(batch * seq_len, d)

    stages = []
    for i in range(depth):
        j = i // 2
        stages.append(("ffn", [(w_ff1_in, i), (w_ff1_out, i)],
                       dict(gain=ln_ff1[i], final_gain=None)))
        if i % 2 == 0:
            stages.append(("conv", [(conv_w_in, j), (conv_w_out, j)],
                           dict(gain=ln_mix[i], layer=j)))
        else:
            stages.append(("attn", [(attn_w_qkv, j), (attn_w_o, j)],
                           dict(gain=ln_mix[i], sinks=attn_sinks[j])))
        stages.append(("ffn", [(w_ff2_in, i), (w_ff2_out, i)],
                       dict(gain=ln_ff2[i], final_gain=ln_final if i == depth - 1 else None)))

    cos_t, sin_t, weights = _rope_tables(positions.reshape(batch * seq_len), stages[0][1])
    for k, (kind, _, args) in enumerate(stages):
        cast_next = stages[k + 1][1] if k + 1 < len(stages) else []
        if kind == "ffn":
            h, weights = _ffn(h, args["gain"], *weights, cast_next,
                              final_gain=args["final_gain"])
        elif kind == "conv":
            h, weights = _conv_mixer(h, args["gain"], weights[0], conv_kernel, weights[1],
                                     args["layer"], seq_len, cast_next)
        else:
            h, weights = _attention(h, args["gain"], cos_t, sin_t, weights[0], args["sinks"],
                                    weights[1], seq_len, cast_next)
    return h.reshape(batch, seq_len, d)
```

```python
import functools
import math

import jax
import jax.numpy as jnp
from jax import lax
from jax.experimental import pallas as pl
from jax.experimental.pallas import tpu as pltpu

RMS_EPS = 1e-6
ROPE_THETA = 10000.0
HEAD_DIM = 64
N_Q_HEADS = 16
N_KV_HEADS = 4
GROUP = N_Q_HEADS // N_KV_HEADS
WINDOW = 128
CONV_WIDTH = 3
LANES = 128
BF16_SUBLANES = 16
MXU_DIM = 256

ROW_TILE = 512
CONV_ROW_TILE = 1024
FFN_ROW_TILE = 1024
FFN_PAIR_ROW_TILE = 512
FFN_CHUNK = 256
NORM_PARTS = 4
ATTN_ROW_TILE = 1024
ATTN_PARTS = 4
SCORE_LOOKAHEAD = 2
LOG2_E = math.log2(math.e)
VMEM_LIMIT_BYTES = 56 * 1024 * 1024

F32 = jnp.float32
BF16 = jnp.bfloat16


def _rmsnorm(x, g):
    ms = jnp.mean(x * x, axis=-1, keepdims=True)
    return (x * lax.rsqrt(ms + RMS_EPS)) * g


def _resident(shape):
    return pl.BlockSpec(shape, lambda i: (0,) * len(shape), pipeline_mode=pl.Buffered(1))


def _resident_layer(shape, layer):
    return pl.BlockSpec((None,) + shape, lambda i: (layer,) + (0,) * len(shape),
                        pipeline_mode=pl.Buffered(1))


def _row_spec(tm, width):
    return pl.BlockSpec((tm, width), lambda i: (i, 0))


def _params():
    return pltpu.CompilerParams(dimension_semantics=("arbitrary",),
                                vmem_limit_bytes=VMEM_LIMIT_BYTES)


def _cast_plan(weights, n_steps):
    in_specs, out_specs, out_shapes = [], [], []
    for w, layer in weights:
        _, rows, cols = w.shape
        block_rows = next(r for r in range(BF16_SUBLANES, rows + 1, BF16_SUBLANES)
                          if rows % r == 0 and rows // r <= n_steps)
        last = rows // block_rows - 1
        in_specs.append(pl.BlockSpec((None, block_rows, cols),
                                     lambda i, layer=layer, last=last: (layer, jnp.minimum(i, last), 0)))
        out_specs.append(pl.BlockSpec((block_rows, cols),
                                      lambda i, last=last: (jnp.minimum(i, last), 0)))
        out_shapes.append(jax.ShapeDtypeStruct((rows, cols), BF16))
    return in_specs, out_specs, out_shapes


def _cast_blocks(src_refs, dst_refs):
    for src, dst in zip(src_refs, dst_refs):
        dst[...] = src[...].astype(BF16)


def _ffn_kernel(*refs, fc, n_ffn, final_norm, n_cast):
    x_ref = refs[0]
    blocks = [refs[1 + 3 * k:4 + 3 * k] for k in range(n_ffn)]
    n_in = 1 + 3 * n_ffn + (1 if final_norm else 0)
    gf_ref = refs[n_in - 1] if final_norm else None
    cast_src = refs[n_in:n_in + n_cast]
    o_ref = refs[n_in + n_cast]
    cast_dst = refs[n_in + n_cast + 1:n_in + 2 * n_cast + 1]
    scratch = refs[n_in + 2 * n_cast + 1:]
    xn_ref, h_ref = scratch[:2]
    mid_ref = scratch[2] if n_ffn > 1 else None
    tm = x_ref.shape[0]
    part = tm // NORM_PARTS

    src_ref = x_ref
    for k, (g_ref, win_ref, wo_ref) in enumerate(blocks):
        last = k == n_ffn - 1
        dst_ref = o_ref if last else mid_ref
        f = wo_ref.shape[0]

        def hidden(rows, lo, hi, win_ref=win_ref, f=f):
            gate = jnp.dot(xn_ref[rows, :], win_ref[:, lo:hi], preferred_element_type=F32)
            up = jnp.dot(xn_ref[rows, :], win_ref[:, f + lo:f + hi], preferred_element_type=F32)
            h_ref[rows, lo:hi] = ((gate / (1.0 + jnp.exp(-gate))) * up).astype(BF16)

        for r in range(0, tm, part):
            rows = slice(r, r + part)
            xn_ref[rows, :] = _rmsnorm(src_ref[rows, :], g_ref[...]).astype(BF16)
            hidden(rows, 0, fc)
        for lo in range(fc, f, fc):
            hidden(slice(None), lo, min(lo + fc, f))
        y = jnp.dot(h_ref[...], wo_ref[...], preferred_element_type=F32)
        out = src_ref[...] + 0.5 * y
        if last and final_norm:
            out = _rmsnorm(out, gf_ref[...])
        dst_ref[...] = out
        src_ref = dst_ref
    _cast_blocks(cast_src, cast_dst)


def _ffn(x, blocks, cast_next, final_gain=None):
    n, d = x.shape
    f = blocks[0][2].shape[0]
    fc = FFN_CHUNK
    tm = FFN_ROW_TILE if len(blocks) == 1 else FFN_PAIR_ROW_TILE
    assert f % MXU_DIM == 0 and fc % MXU_DIM == 0 and n % tm == 0
    final_norm = final_gain is not None
    in_specs = [_row_spec(tm, d)]
    args = [x]
    for gain, w_in, w_out in blocks:
        in_specs += [_resident((1, d)), _resident((d, 2 * f)), _resident((f, d))]
        args += [gain.reshape(1, d), w_in, w_out]
    if final_norm:
        in_specs.append(_resident((1, d)))
        args.append(final_gain.reshape(1, d))
    cast_in, cast_out, cast_shapes = _cast_plan(cast_next, n // tm)
    scratch = [pltpu.VMEM((tm, d), BF16), pltpu.VMEM((tm, f), BF16)]
    if len(blocks) > 1:
        scratch.append(pltpu.VMEM((tm, d), F32))
    out, *casts = pl.pallas_call(
        functools.partial(_ffn_kernel, fc=fc, n_ffn=len(blocks), final_norm=final_norm,
                          n_cast=len(cast_next)),
        grid=(n // tm,),
        in_specs=in_specs + cast_in,
        out_specs=[_row_spec(tm, d)] + cast_out,
        out_shape=[jax.ShapeDtypeStruct((n, d), F32)] + cast_shapes,
        scratch_shapes=scratch,
        compiler_params=_params(),
        name="ffn_final" if final_norm else ("ffn" if len(blocks) == 1 else "ffn_pair"),
    )(*args, *(w for w, _ in cast_next))
    return out, casts


def _conv_kernel(*refs, n_chunks, cc, tiles_per_seq, n_cast):
    x_ref, g_ref, win_ref, ck_ref, wout_ref = refs[:5]
    cast_src = refs[5:5 + n_cast]
    o_ref = refs[5 + n_cast]
    cast_dst = refs[6 + n_cast:6 + 2 * n_cast]
    xn_ref, hc_ref, carry_ref = refs[6 + 2 * n_cast:]
    tm = x_ref.shape[0]
    i = pl.program_id(0)

    @pl.when(i % tiles_per_seq == 0)
    def _():
        carry_ref[...] = jnp.zeros_like(carry_ref)

    d = n_chunks * cc
    xn_ref[...] = _rmsnorm(x_ref[...], g_ref[...]).astype(BF16)
    row = lax.broadcasted_iota(jnp.int32, (tm, cc), 0)
    for j in range(n_chunks):
        cols = slice(j * cc, (j + 1) * cc)
        gate_b, gate_c, u = (
            jnp.dot(xn_ref[...], win_ref[:, part * d + j * cc:part * d + (j + 1) * cc],
                    preferred_element_type=F32) for part in range(3))
        z = gate_c * u
        prev = carry_ref[j]
        carry_ref[j] = z[tm - 8:, :]
        p1 = jnp.broadcast_to(prev[7:8, :], (tm, cc))
        p2 = jnp.broadcast_to(prev[6:7, :], (tm, cc))
        z1 = jnp.where(row == 0, p1, pltpu.roll(z, 1, 0))
        z2 = jnp.where(row == 0, p2, jnp.where(row == 1, p1, pltpu.roll(z, 2, 0)))
        k = ck_ref[:, cols]
        conv = k[0:1, :] * z2
        conv = conv + k[1:2, :] * z1
        conv = conv + k[2:3, :] * z
        hc_ref[:, cols] = (gate_b * conv).astype(BF16)
    y = jnp.dot(hc_ref[...], wout_ref[...], preferred_element_type=F32)
    o_ref[...] = x_ref[...] + y
    _cast_blocks(cast_src, cast_dst)


def _conv_mixer(x, gain, w_in, conv_k, w_out, layer, seq_len, cast_next):
    n, d = x.shape
    tm = CONV_ROW_TILE
    cc = MXU_DIM
    n_chunks = d // cc
    assert seq_len % tm == 0 and n % seq_len == 0 and CONV_WIDTH - 1 <= 8
    cast_in, cast_out, cast_shapes = _cast_plan(cast_next, n // tm)
    out, *casts = pl.pallas_call(
        functools.partial(_conv_kernel, n_chunks=n_chunks, cc=cc, tiles_per_seq=seq_len // tm,
                          n_cast=len(cast_next)),
        grid=(n // tm,),
        in_specs=[_row_spec(tm, d), _resident((1, d)), _resident((d, 3 * d)),
                  _resident_layer((CONV_WIDTH, d), layer), _resident((d, d))] + cast_in,
        out_specs=[_row_spec(tm, d)] + cast_out,
        out_shape=[jax.ShapeDtypeStruct((n, d), F32)] + cast_shapes,
        scratch_shapes=[pltpu.VMEM((tm, d), BF16), pltpu.VMEM((tm, d), BF16),
                        pltpu.VMEM((n_chunks, 8, cc), F32)],
        compiler_params=_params(),
        name="conv_mixer",
    )(x, gain.reshape(1, d), w_in, conv_k, w_out, *(w for w, _ in cast_next))
    return out, casts


def _rope_kernel(*refs, n_cast):
    pos_ref, invf_ref = refs[:2]
    cast_src = refs[2:2 + n_cast]
    cos_ref, sin_ref = refs[2 + n_cast:4 + n_cast]
    cast_dst = refs[4 + n_cast:]
    _cast_blocks(cast_src, cast_dst)
    n_freq = HEAD_DIM // 2
    ang = pos_ref[...].astype(F32) * invf_ref[...]
    lane = lax.broadcasted_iota(jnp.int32, ang.shape, 1)
    sign = jnp.where((lane & n_freq) == 0, -1.0, 1.0)
    groups = LANES // n_freq
    group = lax.shift_right_logical(lane, n_freq.bit_length() - 1)
    for dense, out_ref, factor in ((jnp.cos(ang), cos_ref, None), (jnp.sin(ang), sin_ref, sign)):
        rolled = [dense] + [pltpu.roll(dense, s * n_freq, 1) for s in range(1, groups)]
        for q in range(groups):
            t = rolled[(groups - 1 - q) % groups]
            for g in range(groups - 1):
                t = jnp.where(group == g, rolled[(g - q) % groups], t)
            out_ref[q] = t if factor is None else t * factor


def _rope_tables(positions_flat, cast_next):
    n = positions_flat.shape[0]
    n_freq = HEAD_DIM // 2
    groups = LANES // n_freq
    rows = n // groups
    tm = min(ROW_TILE, rows)
    assert n % groups == 0 and rows % tm == 0
    inv_freq = ROPE_THETA ** (-jnp.arange(0, HEAD_DIM, 2, dtype=F32) / HEAD_DIM)
    invf = jnp.tile(inv_freq, groups).reshape(1, LANES)
    pos = jnp.repeat(positions_flat.reshape(groups, rows).T, n_freq, axis=1)
    out_spec = pl.BlockSpec((groups, tm, LANES), lambda i: (0, i, 0))
    cast_in, cast_out, cast_shapes = _cast_plan(cast_next, rows // tm)
    cos_t, sin_t, *casts = pl.pallas_call(
        functools.partial(_rope_kernel, n_cast=len(cast_next)),
        grid=(rows // tm,),
        in_specs=[_row_spec(tm, LANES), _resident((1, LANES))] + cast_in,
        out_specs=[out_spec, out_spec] + cast_out,
        out_shape=[jax.ShapeDtypeStruct((groups, rows, LANES), F32)] * 2 + cast_shapes,
        compiler_params=_params(),
        name="rope_tables",
    )(pos, invf, *(w for w, _ in cast_next))
    return cos_t.reshape(n, LANES), sin_t.reshape(n, LANES), casts


def _transpose_blocks(t):
    return jnp.concatenate([t[r:r + WINDOW, :].T for r in range(0, t.shape[0], WINDOW)], axis=1)


def _attn_kernel(*refs, tiles_per_seq, n_parts, n_cast):
    sinks_ref, x_ref, g_ref, cos_ref, sin_ref, wqkv_ref, wo_ref = refs[:7]
    cast_src = refs[7:7 + n_cast]
    o_ref = refs[7 + n_cast]
    cast_dst = refs[8 + n_cast:8 + 2 * n_cast]
    xn_ref, q_ref, ka_ref, kb_ref, vat_ref, vbt_ref, att_ref = refs[8 + 2 * n_cast:]
    tm, d = x_ref.shape
    part = tm // n_parts
    blocks_per_part = part // WINDOW
    n_q_slabs = d // LANES
    n_kv_slabs = N_KV_HEADS * HEAD_DIM // LANES
    i = pl.program_id(0)
    first = i % tiles_per_seq == 0

    @pl.when(first)
    def _():
        for r in (ka_ref, kb_ref):
            r[:, 0:WINDOW, :] = jnp.zeros((N_KV_HEADS, WINDOW, LANES), BF16)
        for r in (vat_ref, vbt_ref):
            r[:, :, 0:WINDOW] = jnp.zeros((N_KV_HEADS, LANES, WINDOW), BF16)

    @pl.when(jnp.logical_not(first))
    def _():
        for r in (ka_ref, kb_ref):
            r[:, 0:WINDOW, :] = r[:, tm:tm + WINDOW, :]
        for r in (vat_ref, vbt_ref):
            r[:, :, 0:WINDOW] = r[:, :, tm:tm + WINDOW]

    lane = lax.broadcasted_iota(jnp.int32, (part, LANES), 1)
    first_half = (lane & (HEAD_DIM // 2)) == 0
    low_head = lane < HEAD_DIM
    zero = jnp.zeros((part, LANES), F32)
    zero_t = jnp.zeros((HEAD_DIM, part), BF16)
    q_scale = LOG2_E / math.sqrt(HEAD_DIM)

    def rope(t, rows):
        partner = jnp.where(first_half,
                            pltpu.roll(t, LANES - HEAD_DIM // 2, 1),
                            pltpu.roll(t, HEAD_DIM // 2, 1))
        return t * cos_ref[rows, :] + partner * sin_ref[rows, :]

    def project_chunk(p, c):
        rows = slice(p * part, (p + 1) * part)
        halo_rows = slice(WINDOW + p * part, WINDOW + (p + 1) * part)
        if c == 0:
            xn_ref[rows, :] = _rmsnorm(x_ref[rows, :], g_ref[...]).astype(BF16)
        t2 = jnp.dot(xn_ref[rows, :], wqkv_ref[:, c * MXU_DIM:(c + 1) * MXU_DIM],
                     preferred_element_type=F32)
        for half in range(MXU_DIM // LANES):
            s = c * (MXU_DIM // LANES) + half
            t = t2[:, half * LANES:(half + 1) * LANES]
            if s < n_q_slabs:
                q_ref[rows, s * LANES:(s + 1) * LANES] = (rope(t, rows) * q_scale).astype(BF16)
            elif s < n_q_slabs + n_kv_slabs:
                ks = s - n_q_slabs
                k_slab = rope(t, rows)
                swapped = pltpu.roll(k_slab, HEAD_DIM, 1)
                ka_ref[2 * ks, halo_rows, :] = jnp.where(low_head, k_slab, zero).astype(BF16)
                kb_ref[2 * ks, halo_rows, :] = jnp.where(low_head, zero, swapped).astype(BF16)
                ka_ref[2 * ks + 1, halo_rows, :] = jnp.where(low_head, swapped, zero).astype(BF16)
                kb_ref[2 * ks + 1, halo_rows, :] = jnp.where(low_head, zero, k_slab).astype(BF16)
            else:
                vs = s - n_q_slabs - n_kv_slabs
                v_t = _transpose_blocks(t).astype(BF16)
                for head, v_head in ((2 * vs, v_t[:HEAD_DIM, :]), (2 * vs + 1, v_t[HEAD_DIM:, :])):
                    vat_ref[head, :, halo_rows] = jnp.concatenate([v_head, zero_t], axis=0)
                    vbt_ref[head, :, halo_rows] = jnp.concatenate([zero_t, v_head], axis=0)

    def out_chunk(p, c):
        rows = slice(p * part, (p + 1) * part)
        cols = slice(c * MXU_DIM, (c + 1) * MXU_DIM)
        y = jnp.dot(att_ref[rows, :], wo_ref[:, cols], preferred_element_type=F32)
        o_ref[rows, cols] = x_ref[rows, cols] + y

    project_chunks = wqkv_ref.shape[1] // MXU_DIM
    out_chunks = d // MXU_DIM

    HALF = WINDOW // 2
    kj = lax.broadcasted_iota(jnp.int32, (WINDOW, LANES), 0)
    ql = lax.broadcasted_iota(jnp.int32, (WINDOW, LANES), 1) & (HALF - 1)
    pair_of_lane = lax.broadcasted_iota(jnp.int32, (1, LANES), 1) < HALF
    neg = jnp.finfo(F32).min
    no_prev = jnp.where(first, WINDOW, 0)

    def band_mask(h, seq_first_block):
        qi = ql + h * HALF
        prev_ok = kj > (qi + no_prev if seq_first_block else qi)
        cur_ok = kj <= qi
        if h == 0:
            return slice(0, WINDOW + HALF), jnp.concatenate([prev_ok, cur_ok[:HALF]], axis=0)
        return slice(HALF, 2 * WINDOW), jnp.concatenate([prev_ok[HALF:], cur_ok], axis=0)

    def scores(b, g):
        keys = slice(b * WINDOW, b * WINDOW + 2 * WINDOW)
        q2 = jnp.concatenate(
            [q_ref[b * WINDOW + h * HALF:b * WINDOW + (h + 1) * HALF, sl * LANES:(sl + 1) * LANES]
             for h in range(2) for sl in (2 * g, 2 * g + 1)], axis=0)
        k2 = jnp.concatenate([ka_ref[g, keys, :], kb_ref[g, keys, :]], axis=0)
        return lax.dot_general(k2, q2, (((1,), (1,)), ((), ())),
                               preferred_element_type=F32)

    def attend(b, g, s_t):
        keys = slice(b * WINDOW, b * WINDOW + 2 * WINDOW)
        p_cols = []
        denoms = []
        for h in range(2):
            active, mask = band_mask(h, b == 0)
            p_col = []
            denom_col = []
            for lo in range(2):
                sink = jnp.where(pair_of_lane, sinks_ref[GROUP * g + lo],
                                 sinks_ref[GROUP * g + 2 + lo]) * LOG2_E
                sc = s_t[lo * 2 * WINDOW + active.start:lo * 2 * WINDOW + active.stop,
                         h * LANES:(h + 1) * LANES]
                sc = jnp.where(mask, sc, neg)
                m = jnp.maximum(jnp.max(sc, axis=0, keepdims=True), sink)
                p = jnp.exp2(sc - m)
                denom_col.append(jnp.sum(p, axis=0, keepdims=True) + jnp.exp2(sink - m))
                p = p.astype(BF16)
                skipped = jnp.zeros((HALF, LANES), BF16)
                p_col.append(jnp.concatenate([p, skipped] if h == 0 else [skipped, p], axis=0))
            p_cols.append(jnp.concatenate(p_col, axis=0))
            denoms.append(denom_col)
        p_t = jnp.concatenate(p_cols, axis=1)
        v2_t = jnp.concatenate([vat_ref[g, :, keys], vbt_ref[g, :, keys]], axis=1)
        o_t = jnp.dot(v2_t, p_t, preferred_element_type=F32)
        for h in range(2):
            cols = slice(h * LANES, (h + 1) * LANES)
            inv = [1.0 / denoms[h][lo] for lo in range(2)]
            o_half = jnp.concatenate([o_t[:HEAD_DIM, cols] * inv[0],
                                      o_t[HEAD_DIM:LANES, cols] * inv[1]], axis=0)
            o_half = o_half.T.astype(BF16)
            rows = slice(b * WINDOW + h * HALF, b * WINDOW + (h + 1) * HALF)
            for pair in range(2):
                slab = 2 * g + pair
                att_ref[rows, slab * LANES:(slab + 1) * LANES] = (
                    o_half[pair * HALF:(pair + 1) * HALF, :])

    units = [(b, g) for b in range(tm // WINDOW) for g in range(N_KV_HEADS)]
    units_per_part = blocks_per_part * N_KV_HEADS
    filler_slots = units_per_part - SCORE_LOOKAHEAD

    for c in range(project_chunks):
        project_chunk(0, c)
    pending = [scores(*u) for u in units[:SCORE_LOOKAHEAD]]
    fillers = []
    for n, (b, g) in enumerate(units):
        p, j = divmod(n, units_per_part)
        if j == 0:
            fillers = []
            if p + 1 < n_parts:
                fillers += [functools.partial(project_chunk, p + 1, c)
                            for c in range(project_chunks)]
            if p >= 1:
                fillers += [functools.partial(out_chunk, p - 1, c) for c in range(out_chunks)]
        if n + SCORE_LOOKAHEAD < len(units):
            pending.append(scores(*units[n + SCORE_LOOKAHEAD]))
        attend(b, g, pending[n])
        pending[n] = None
        if j < filler_slots:
            lo_f = len(fillers) * j // filler_slots
            hi_f = len(fillers) * (j + 1) // filler_slots
            for filler in fillers[lo_f:hi_f]:
                filler()
    for c in range(out_chunks):
        out_chunk(n_parts - 1, c)
    _cast_blocks(cast_src, cast_dst)


def _attention(x, gain, cos_t, sin_t, w_qkv, sinks, w_o, seq_len, cast_next):
    n, d = x.shape
    tm = ATTN_ROW_TILE
    n_parts = ATTN_PARTS
    qkv_w = w_qkv.shape[1]
    assert seq_len % tm == 0 and tm % (n_parts * WINDOW) == 0 and d == N_Q_HEADS * HEAD_DIM
    assert qkv_w % MXU_DIM == 0 and (tm // n_parts // WINDOW) * N_KV_HEADS > SCORE_LOOKAHEAD
    k_halo = pltpu.VMEM((N_KV_HEADS, tm + WINDOW, LANES), BF16)
    vt_halo = pltpu.VMEM((N_KV_HEADS, LANES, tm + WINDOW), BF16)
    cast_in, cast_out, cast_shapes = _cast_plan(cast_next, n // tm)
    out, *casts = pl.pallas_call(
        functools.partial(_attn_kernel, tiles_per_seq=seq_len // tm, n_parts=n_parts,
                          n_cast=len(cast_next)),
        grid=(n // tm,),
        in_specs=[pl.BlockSpec(memory_space=pltpu.SMEM),
                  _row_spec(tm, d), _resident((1, d)), _row_spec(tm, LANES), _row_spec(tm, LANES),
                  _resident((d, qkv_w)), _resident((d, d))] + cast_in,
        out_specs=[_row_spec(tm, d)] + cast_out,
        out_shape=[jax.ShapeDtypeStruct((n, d), F32)] + cast_shapes,
        scratch_shapes=[pltpu.VMEM((tm, d), BF16), pltpu.VMEM((tm, d), BF16),
                        k_halo, k_halo, vt_halo, vt_halo,
                        pltpu.VMEM((tm, d), BF16)],
        compiler_params=_params(),
        name="swa_attention",
    )(sinks, x, gain.reshape(1, d), cos_t, sin_t, w_qkv, w_o, *(w for w, _ in cast_next))
    return out, casts


def kernel(x, positions, ln_ff1, w_ff1_in, w_ff1_out, ln_mix, ln_ff2, w_ff2_in, w_ff2_out,
           conv_w_in, conv_kernel, conv_w_out, attn_w_qkv, attn_sinks, attn_w_o, ln_final):
    batch, seq_len, d = x.shape
    depth = ln_ff1.shape[0]
    h = x.reshape(batch * seq_len, d)

    stages = []
    ffn = []

    def flush(final_gain=None):
        if ffn:
            stages.append(("ffn", [w for ws, _ in ffn for w in ws],
                           dict(gains=[g for _, g in ffn], final_gain=final_gain)))
            ffn.clear()

    for i in range(depth):
        j = i // 2
        ffn.append(([(w_ff1_in, i), (w_ff1_out, i)], ln_ff1[i]))
        flush()
        if i % 2 == 0:
            stages.append(("conv", [(conv_w_in, j), (conv_w_out, j)],
                           dict(gain=ln_mix[i], layer=j)))
        else:
            stages.append(("attn", [(attn_w_qkv, j), (attn_w_o, j)],
                           dict(gain=ln_mix[i], sinks=attn_sinks[j])))
        ffn.append(([(w_ff2_in, i), (w_ff2_out, i)], ln_ff2[i]))
    flush(final_gain=ln_final)

    cos_t, sin_t, weights = _rope_tables(positions.reshape(batch * seq_len), stages[0][1])
    for k, (kind, _, args) in enumerate(stages):
        cast_next = stages[k + 1][1] if k + 1 < len(stages) else []
        if kind == "ffn":
            blocks = [(g, weights[2 * m], weights[2 * m + 1]) for m, g in enumerate(args["gains"])]
            h, weights = _ffn(h, blocks, cast_next, final_gain=args["final_gain"])
        elif kind == "conv":
            h, weights = _conv_mixer(h, args["gain"], weights[0], conv_kernel, weights[1],
                                     args["layer"], seq_len, cast_next)
        else:
            h, weights = _attention(h, args["gain"], cos_t, sin_t, weights[0], args["sinks"],
                                    weights[1], seq_len, cast_next)
    return h.reshape(batch, seq_len, d)
```
